```python
import math
import jax, jax.numpy as jnp
from jax import lax
import numpy as np

D_MODEL = 2048
BATCH = 2
SEQ = 4096
DEPTH = 4
DEC_BATCH = 128
DEC_SEQ = 1
PAST_LEN = 8192
PAGE_SIZE = 128

N_MIXERS = 3
RMS_EPS = 1e-6
ROPE_THETA = 10000.0
D_FF = 4 * D_MODEL

SWA_HEAD_DIM = 64
SWA_HEADS = D_MODEL // SWA_HEAD_DIM
SWA_KV_HEADS = SWA_HEADS // 8
SWA_GROUP = SWA_HEADS // SWA_KV_HEADS
SWA_WINDOW = 128
SWA_Q_DIM = SWA_HEADS * SWA_HEAD_DIM
SWA_KV_DIM = SWA_KV_HEADS * SWA_HEAD_DIM
SWA_QKV_DIM = SWA_Q_DIM + 2 * SWA_KV_DIM
SWA_SCALE = SWA_HEAD_DIM ** -0.5

GDN_HEAD_DIM = 128
GDN_QK_HEADS = D_MODEL // GDN_HEAD_DIM
GDN_V_HEADS = 2 * GDN_QK_HEADS
GDN_V_PER_QK = GDN_V_HEADS // GDN_QK_HEADS
GDN_KEY_DIM = GDN_QK_HEADS * GDN_HEAD_DIM
GDN_VAL_DIM = GDN_V_HEADS * GDN_HEAD_DIM
GDN_CONV_DIM = 2 * GDN_KEY_DIM + GDN_VAL_DIM
GDN_IN_DIM = GDN_CONV_DIM + GDN_VAL_DIM + 2 * GDN_V_HEADS
GDN_CONV = 4
GDN_CHUNK = 64
GDN_SCALE = GDN_HEAD_DIM ** -0.5

MLA_HEADS = 16
MLA_Q_LORA = 512
MLA_KV_LORA = 512
MLA_NOPE = 128
MLA_ROPE = 64
MLA_V = 128
MLA_IN_DIM = MLA_Q_LORA + MLA_KV_LORA + MLA_ROPE
MLA_Q_BLOCK = 128
MLA_SCALE = (MLA_NOPE + MLA_ROPE) ** -0.5

N_SWA_LAYERS = (DEPTH + 2) // 3
N_GDN_LAYERS = (DEPTH + 1) // 3
N_MLA_LAYERS = DEPTH // 3

kernel_name = 'hybrid_swa_gdn_mla_decode_step'


def _rms_norm(x, g):
    xf = x.astype(jnp.float32)
    y = xf * lax.rsqrt(jnp.mean(xf * xf, axis=-1, keepdims=True) + RMS_EPS)
    return (y * g.astype(jnp.float32)).astype(x.dtype)


def _l2norm(x):
    return x * lax.rsqrt(jnp.sum(x * x, axis=-1, keepdims=True) + RMS_EPS)


def _rope(x, pos):
    d = x.shape[-1]
    half = d // 2
    inv_freq = ROPE_THETA ** (-jnp.arange(half, dtype=jnp.float32) * (2.0 / d))
    ang = pos.astype(jnp.float32)[:, None] * inv_freq[None, :]
    cos = jnp.cos(ang)[:, None, :]
    sin = jnp.sin(ang)[:, None, :]
    xf = x.astype(jnp.float32)
    x1, x2 = xf[..., :half], xf[..., half:]
    return jnp.concatenate([x1 * cos - x2 * sin, x2 * cos + x1 * sin], axis=-1).astype(x.dtype)


def _softmax_with_sink(s, mask, sink):
    s = jnp.where(mask, s, -jnp.inf)
    m = jnp.maximum(jnp.max(s, axis=-1, keepdims=True), sink)
    p = jnp.exp(s - m)
    return p / (jnp.sum(p, axis=-1, keepdims=True) + jnp.exp(sink - m))


def _ffn(h, w_up, w_down):
    u = jax.nn.relu(h @ w_up)
    return (u * u) @ w_down


def _swa_qkv(h, w_qkv, b_qkv, pos):
    B, T, _ = h.shape
    qkv = h @ w_qkv + b_qkv
    q = _rope(qkv[..., :SWA_Q_DIM].reshape(B, T, SWA_HEADS, SWA_HEAD_DIM), pos)
    k = _rope(qkv[..., SWA_Q_DIM:SWA_Q_DIM + SWA_KV_DIM].reshape(B, T, SWA_KV_HEADS, SWA_HEAD_DIM), pos)
    v = qkv[..., SWA_Q_DIM + SWA_KV_DIM:].reshape(B, T, SWA_KV_HEADS, SWA_HEAD_DIM)
    return q.reshape(B, T, SWA_KV_HEADS, SWA_GROUP, SWA_HEAD_DIM), k, v


def _swa_prompt(h, w_qkv, b_qkv, w_o, sinks):
    B, T, _ = h.shape
    W = SWA_WINDOW
    nb = T // W
    pos = jnp.arange(T, dtype=jnp.int32)
    q, k, v = _swa_qkv(h, w_qkv, b_qkv, pos)
    qb = q.reshape(B, nb, W, SWA_KV_HEADS, SWA_GROUP, SWA_HEAD_DIM)

    def band(x):
        xb = x.reshape(B, nb, W, SWA_KV_HEADS, SWA_HEAD_DIM)
        prev = jnp.pad(xb, ((0, 0), (1, 0), (0, 0), (0, 0), (0, 0)))[:, :nb]
        return jnp.concatenate([prev, xb], axis=2)

    kb, vb = band(k), band(v)
    qi = jnp.arange(W)[:, None]
    kj = jnp.arange(2 * W)[None, :]
    rel = qi + W - kj
    blk = jnp.arange(nb)[:, None, None]
    mask = (rel >= 0) & (rel < W) & (blk * W - W + kj >= 0)
    s = jnp.einsum('bnqkgd,bnskd->bnkgqs', qb, kb).astype(jnp.float32) * SWA_SCALE
    sink = sinks.astype(jnp.float32).reshape(SWA_KV_HEADS, SWA_GROUP, 1, 1)
    p = _softmax_with_sink(s, mask[None, :, None, None], sink)
    o = jnp.einsum('bnkgqs,bnskd->bnqkgd', p.astype(vb.dtype), vb).reshape(B, T, SWA_Q_DIM)
    rows = min(W, T)
    return o @ w_o, k[:, T - rows:], v[:, T - rows:]


def _swa_sample(h, ck, cv, past_len, w_qkv, b_qkv, w_o, sinks):
    B, T, _ = h.shape
    P = ck.shape[1]
    pos = past_len + jnp.arange(T, dtype=jnp.int32)
    q, k, v = _swa_qkv(h, w_qkv, b_qkv, pos)
    kk = jnp.concatenate([ck.astype(k.dtype), k], axis=1)
    vv = jnp.concatenate([cv.astype(v.dtype), v], axis=1)
    kpos = jnp.concatenate([past_len - P + jnp.arange(P, dtype=jnp.int32), pos])
    rel = pos[:, None] - kpos[None, :]
    mask = (rel >= 0) & (rel < SWA_WINDOW)
    s = jnp.einsum('btkgd,bskd->bkgts', q, kk).astype(jnp.float32) * SWA_SCALE
    sink = sinks.astype(jnp.float32).reshape(SWA_KV_HEADS, SWA_GROUP, 1, 1)
    p = _softmax_with_sink(s, mask, sink)
    o = jnp.einsum('bkgts,bskd->btkgd', p.astype(vv.dtype), vv).reshape(B, T, SWA_Q_DIM)
    return o @ w_o, kk[:, T:], vv[:, T:]


def _gdn_split(h, w_in):
    proj = h @ w_in
    c0 = GDN_CONV_DIM
    c1 = c0 + GDN_VAL_DIM
    c2 = c1 + GDN_V_HEADS
    return proj[..., :c0], proj[..., c0:c1], proj[..., c1:c2], proj[..., c2:]


def _gdn_prepare(conv_in, conv_w, b, a, A_log, dt_bias, T):
    acc = conv_in[:, 0:T] * conv_w[0]
    for j in range(1, GDN_CONV):
        acc = acc + conv_in[:, j:j + T] * conv_w[j]
    c = jax.nn.silu(acc).astype(jnp.float32)
    B = c.shape[0]
    q = _l2norm(c[..., :GDN_KEY_DIM].reshape(B, T, GDN_QK_HEADS, GDN_HEAD_DIM))
    k = _l2norm(c[..., GDN_KEY_DIM:2 * GDN_KEY_DIM].reshape(B, T, GDN_QK_HEADS, GDN_HEAD_DIM))
    v = c[..., 2 * GDN_KEY_DIM:].reshape(B, T, GDN_V_HEADS, GDN_HEAD_DIM)
    q = jnp.repeat(q, GDN_V_PER_QK, axis=2) * GDN_SCALE
    k = jnp.repeat(k, GDN_V_PER_QK, axis=2)
    beta = jax.nn.sigmoid(b.astype(jnp.float32))
    g = -jnp.exp(A_log.astype(jnp.float32)) * jax.nn.softplus(a.astype(jnp.float32) + dt_bias.astype(jnp.float32))
    return q, k, v, g, beta


def _gdn_chunked(q, k, v, g, beta, S0):
    B, T, H, dk = q.shape
    dv = v.shape[-1]
    C = GDN_CHUNK
    n = T // C

    def chunks(x):
        x = x.reshape(B, n, C, H, *x.shape[3:])
        return jnp.moveaxis(x, (1, 3), (0, 2))

    qc, kc, vc, bc = chunks(q), chunks(k), chunks(v), chunks(beta)
    gc = jnp.cumsum(chunks(g), axis=-1)
    tri = jnp.tril(jnp.ones((C, C), dtype=bool))
    strict = jnp.tril(jnp.ones((C, C), dtype=bool), -1)
    diff = gc[..., :, None] - gc[..., None, :]
    decay = jnp.where(tri, jnp.exp(jnp.where(tri, diff, 0.0)), 0.0)
    kbeta = kc * bc[..., None]
    L = jnp.where(strict, jnp.einsum('nbhid,nbhjd->nbhij', kbeta, kc) * decay, 0.0)
    eye = jnp.eye(C, dtype=jnp.float32)
    t_inv = lax.linalg.triangular_solve(eye + L, jnp.broadcast_to(eye, L.shape),
                                        left_side=True, lower=True, unit_diagonal=True)
    u = jnp.einsum('nbhij,nbhje->nbhie', t_inv, vc * bc[..., None])
    w = jnp.einsum('nbhij,nbhjd->nbhid', t_inv, kbeta * jnp.exp(gc)[..., None])
    qk = jnp.where(tri, jnp.einsum('nbhid,nbhjd->nbhij', qc, kc) * decay, 0.0)

    def step(S, xs):
        q_i, k_i, u_i, w_i, qk_i, g_i = xs
        v_new = u_i - jnp.einsum('bhcd,bhde->bhce', w_i, S)
        o_i = (jnp.einsum('bhcd,bhde->bhce', q_i * jnp.exp(g_i)[..., None], S)
               + jnp.einsum('bhij,bhje->bhie', qk_i, v_new))
        g_last = g_i[..., -1:]
        S = (S * jnp.exp(g_last)[..., None]
             + jnp.einsum('bhcd,bhce->bhde', k_i * jnp.exp(g_last - g_i)[..., None], v_new))
        return S, o_i

    S_fin, o = lax.scan(step, S0, (qc, kc, u, w, qk, gc))
    o = jnp.moveaxis(o, (0, 2), (1, 3)).reshape(B, T, H, dv)
    return o, S_fin


def _gdn_recurrent(q, k, v, g, beta, S0):
    def step(S, xs):
        q_t, k_t, v_t, g_t, b_t = xs
        S = S * jnp.exp(g_t)[..., None, None]
        delta = (v_t - jnp.einsum('bhd,bhde->bhe', k_t, S)) * b_t[..., None]
        S = S + jnp.einsum('bhd,bhe->bhde', k_t, delta)
        return S, jnp.einsum('bhd,bhde->bhe', q_t, S)

    xs = (jnp.moveaxis(q, 1, 0), jnp.moveaxis(k, 1, 0), jnp.moveaxis(v, 1, 0),
          jnp.moveaxis(g, 1, 0), jnp.moveaxis(beta, 1, 0))
    S_fin, o = lax.scan(step, S0, xs)
    return jnp.moveaxis(o, 0, 1), S_fin


def _gdn_out(o, z, norm_w, w_out, dtype):
    B, T = o.shape[:2]
    gate = jax.nn.silu(z.astype(jnp.float32).reshape(B, T, GDN_V_HEADS, GDN_HEAD_DIM))
    y = _rms_norm(o, norm_w) * gate
    return y.reshape(B, T, GDN_VAL_DIM).astype(dtype) @ w_out


def _gdn_prompt(h, w_in, conv_w, A_log, dt_bias, norm_w, w_out):
    B, T, _ = h.shape
    mixed, z, b, a = _gdn_split(h, w_in)
    conv_in = jnp.pad(mixed, ((0, 0), (GDN_CONV - 1, 0), (0, 0)))
    q, k, v, g, beta = _gdn_prepare(conv_in, conv_w, b, a, A_log, dt_bias, T)
    S0 = jnp.zeros((B, GDN_V_HEADS, GDN_HEAD_DIM, GDN_HEAD_DIM), jnp.float32)
    o, S = _gdn_chunked(q, k, v, g, beta, S0)
    return _gdn_out(o, z, norm_w, w_out, h.dtype), S.astype(h.dtype), conv_in[:, T:]


def _gdn_sample(h, ssm, conv_state, w_in, conv_w, A_log, dt_bias, norm_w, w_out):
    B, T, _ = h.shape
    mixed, z, b, a = _gdn_split(h, w_in)
    conv_in = jnp.concatenate([conv_state.astype(mixed.dtype), mixed], axis=1)
    q, k, v, g, beta = _gdn_prepare(conv_in, conv_w, b, a, A_log, dt_bias, T)
    o, S = _gdn_recurrent(q, k, v, g, beta, ssm.astype(jnp.float32))
    return _gdn_out(o, z, norm_w, w_out, h.dtype), S.astype(ssm.dtype), conv_in[:, T:]


def _mla_project(h, pos, w_in, q_norm, w_uq, kv_norm):
    B, T, _ = h.shape
    proj = h @ w_in
    cq = _rms_norm(proj[..., :MLA_Q_LORA], q_norm)
    q = (cq @ w_uq).reshape(B, T, MLA_HEADS, MLA_NOPE + MLA_ROPE)
    q_nope = q[..., :MLA_NOPE]
    q_rope = _rope(q[..., MLA_NOPE:], pos)
    c_kv = _rms_norm(proj[..., MLA_Q_LORA:MLA_Q_LORA + MLA_KV_LORA], kv_norm)
    k_rope = _rope(proj[..., MLA_Q_LORA + MLA_KV_LORA:][:, :, None, :], pos)[:, :, 0]
    return q_nope, q_rope, c_kv, k_rope


def _mla_prompt(h, w_in, q_norm, w_uq, kv_norm, w_ukv, w_o):
    B, T, _ = h.shape
    pos = jnp.arange(T, dtype=jnp.int32)
    q_nope, q_rope, c_kv, k_rope = _mla_project(h, pos, w_in, q_norm, w_uq, kv_norm)
    kv = (c_kv @ w_ukv).reshape(B, T, MLA_HEADS, MLA_NOPE + MLA_V)
    k_nope, v = kv[..., :MLA_NOPE], kv[..., MLA_NOPE:]
    nqb = T // MLA_Q_BLOCK

    def to_blocks(x):
        return jnp.moveaxis(x.reshape(B, nqb, MLA_Q_BLOCK, *x.shape[2:]), 1, 0)

    def attend(args):
        blk, qn, qr = args
        s = (jnp.einsum('bqhd,bshd->bhqs', qn, k_nope)
             + jnp.einsum('bqhr,bsr->bhqs', qr, k_rope)).astype(jnp.float32) * MLA_SCALE
        qpos = blk * MLA_Q_BLOCK + jnp.arange(MLA_Q_BLOCK)
        mask = pos[None, :] <= qpos[:, None]
        p = jax.nn.softmax(jnp.where(mask, s, -jnp.inf), axis=-1)
        return jnp.einsum('bhqs,bshd->bqhd', p.astype(v.dtype), v)

    o = lax.map(attend, (jnp.arange(nqb), to_blocks(q_nope), to_blocks(q_rope)))
    o = jnp.moveaxis(o, 0, 1).reshape(B, T, MLA_HEADS * MLA_V)
    return o @ w_o, c_kv, k_rope


def _mla_sample(h, lat_pages, rope_pages, past_len, w_in, q_norm, w_uq, kv_norm, w_ukv, w_o):
    B, T, _ = h.shape
    pos = past_len + jnp.arange(T, dtype=jnp.int32)
    q_nope, q_rope, c_kv, k_rope = _mla_project(h, pos, w_in, q_norm, w_uq, kv_norm)
    past_lat = lat_pages.reshape(B, past_len, MLA_KV_LORA).astype(h.dtype)
    past_rope = rope_pages.reshape(B, past_len, MLA_ROPE).astype(h.dtype)
    w = w_ukv.reshape(MLA_KV_LORA, MLA_HEADS, MLA_NOPE + MLA_V)
    w_uk, w_uv = w[..., :MLA_NOPE], w[..., MLA_NOPE:]
    q_lat = jnp.einsum('bthd,lhd->bthl', q_nope, w_uk)
    s_past = jnp.einsum('bthl,bsl->bhts', q_lat, past_lat) + jnp.einsum('bthr,bsr->bhts', q_rope, past_rope)
    s_new = jnp.einsum('bthl,bsl->bhts', q_lat, c_kv) + jnp.einsum('bthr,bsr->bhts', q_rope, k_rope)
    causal = jnp.tril(jnp.ones((T, T), dtype=bool))
    s_new = jnp.where(causal, s_new.astype(jnp.float32), -jnp.inf)
    s = jnp.concatenate([s_past.astype(jnp.float32), s_new], axis=-1) * MLA_SCALE
    p = jax.nn.softmax(s, axis=-1).astype(h.dtype)
    o_lat = (jnp.einsum('bhts,bsl->bthl', p[..., :past_len], past_lat)
             + jnp.einsum('bhts,bsl->bthl', p[..., past_len:], c_kv))
    o = jnp.einsum('bthl,lhd->bthd', o_lat, w_uv).reshape(B, T, MLA_HEADS * MLA_V)
    return o @ w_o, c_kv, k_rope


def setup_inputs(seed: int = 0) -> dict:
    key = jax.random.key(seed)
    ks = jax.random.split(key, 32)
    f32 = jnp.float32

    def nrm(i, shape, scale):
        return jax.random.normal(ks[i], shape, f32) * scale

    n_pages = PAST_LEN // PAGE_SIZE
    n_pool = (DEC_BATCH * n_pages * 5) // 4
    win_rows = min(SWA_WINDOW, PAST_LEN)
    page_table = jax.random.permutation(ks[8], n_pool)[:DEC_BATCH * n_pages].reshape(DEC_BATCH, n_pages).astype(jnp.int32)
    dt = jnp.exp(jax.random.uniform(ks[22], (N_GDN_LAYERS, GDN_V_HEADS), f32, math.log(1e-3), math.log(1e-1)))
    return {
        'x_prompt': nrm(0, (BATCH, SEQ, D_MODEL), 1.0),
        'x_sample': nrm(1, (DEC_BATCH, DEC_SEQ, D_MODEL), 1.0),
        'cache_swa_k': nrm(2, (N_SWA_LAYERS, DEC_BATCH, win_rows, SWA_KV_HEADS, SWA_HEAD_DIM), 1.0),
        'cache_swa_v': nrm(3, (N_SWA_LAYERS, DEC_BATCH, win_rows, SWA_KV_HEADS, SWA_HEAD_DIM), 1.0),
        'state_gdn_ssm': nrm(4, (N_GDN_LAYERS, DEC_BATCH, GDN_V_HEADS, GDN_HEAD_DIM, GDN_HEAD_DIM), 0.05),
        'state_gdn_conv': nrm(5, (N_GDN_LAYERS, DEC_BATCH, GDN_CONV - 1, GDN_CONV_DIM), 1.0),
        'cache_mla_latent': nrm(6, (N_MLA_LAYERS, n_pool, PAGE_SIZE, MLA_KV_LORA), 1.0),
        'cache_mla_krope': nrm(7, (N_MLA_LAYERS, n_pool, PAGE_SIZE, MLA_ROPE), 1.0),
        'page_table': page_table,
        'norm_mix_pre': 1.0 + nrm(9, (DEPTH, D_MODEL), 0.02),
        'norm_mix_post': 1.0 + nrm(10, (DEPTH, D_MODEL), 0.02),
        'norm_ffn_pre': 1.0 + nrm(11, (DEPTH, D_MODEL), 0.02),
        'norm_ffn_post': 1.0 + nrm(12, (DEPTH, D_MODEL), 0.02),
        'ffn_w_up': nrm(13, (DEPTH, D_MODEL, D_FF), D_MODEL ** -0.5),
        'ffn_w_down': nrm(14, (DEPTH, D_FF, D_MODEL), D_FF ** -0.5),
        'swa_w_qkv': nrm(15, (N_SWA_LAYERS, D_MODEL, SWA_QKV_DIM), D_MODEL ** -0.5),
        'swa_b_qkv': nrm(16, (N_SWA_LAYERS, SWA_QKV_DIM), 0.02),
        'swa_w_o': nrm(17, (N_SWA_LAYERS, SWA_Q_DIM, D_MODEL), SWA_Q_DIM ** -0.5),
        'swa_sinks': nrm(18, (N_SWA_LAYERS, SWA_HEADS), 0.5),
        'gdn_w_in': nrm(19, (N_GDN_LAYERS, D_MODEL, GDN_IN_DIM), D_MODEL ** -0.5),
        'gdn_conv_w': nrm(20, (N_GDN_LAYERS, GDN_CONV, GDN_CONV_DIM), GDN_CONV ** -0.5),
        'gdn_A_log': jnp.log(jax.random.uniform(ks[21], (N_GDN_LAYERS, GDN_V_HEADS), f32, 1.0, 16.0)),
        'gdn_dt_bias': dt + jnp.log(-jnp.expm1(-dt)),
        'gdn_norm': 1.0 + nrm(23, (N_GDN_LAYERS, GDN_HEAD_DIM), 0.02),
        'gdn_w_out': nrm(24, (N_GDN_LAYERS, GDN_VAL_DIM, D_MODEL), GDN_VAL_DIM ** -0.5),
        'mla_w_in': nrm(25, (N_MLA_LAYERS, D_MODEL, MLA_IN_DIM), D_MODEL ** -0.5),
        'mla_q_norm': 1.0 + nrm(26, (N_MLA_LAYERS, MLA_Q_LORA), 0.02),
        'mla_w_uq': nrm(27, (N_MLA_LAYERS, MLA_Q_LORA, MLA_HEADS * (MLA_NOPE + MLA_ROPE)), MLA_Q_LORA ** -0.5),
        'mla_kv_norm': 1.0 + nrm(28, (N_MLA_LAYERS, MLA_KV_LORA), 0.02),
        'mla_w_ukv': nrm(29, (N_MLA_LAYERS, MLA_KV_LORA, MLA_HEADS * (MLA_NOPE + MLA_V)), MLA_KV_LORA ** -0.5),
        'mla_w_o': nrm(30, (N_MLA_LAYERS, MLA_HEADS * MLA_V, D_MODEL), (MLA_HEADS * MLA_V) ** -0.5),
    }


def reference(x_prompt, x_sample, cache_swa_k, cache_swa_v, state_gdn_ssm, state_gdn_conv,
              cache_mla_latent, cache_mla_krope, page_table,
              norm_mix_pre, norm_mix_post, norm_ffn_pre, norm_ffn_post, ffn_w_up, ffn_w_down,
              swa_w_qkv, swa_b_qkv, swa_w_o, swa_sinks,
              gdn_w_in, gdn_conv_w, gdn_A_log, gdn_dt_bias, gdn_norm, gdn_w_out,
              mla_w_in, mla_q_norm, mla_w_uq, mla_kv_norm, mla_w_ukv, mla_w_o):
    past_len = page_table.shape[1] * cache_mla_latent.shape[2]
    hp, hs = x_prompt, x_sample
    swa_kp, swa_vp, swa_ks, swa_vs = [], [], [], []
    gdn_sp, gdn_cp, gdn_ss, gdn_cs = [], [], [], []
    mla_lp, mla_rp, mla_ls, mla_rs = [], [], [], []
    for layer in range(DEPTH):
        kind, j = layer % N_MIXERS, layer // N_MIXERS
        ap = _rms_norm(hp, norm_mix_pre[layer])
        a_s = _rms_norm(hs, norm_mix_pre[layer])
        if kind == 0:
            mp, kp, vp = _swa_prompt(ap, swa_w_qkv[j], swa_b_qkv[j], swa_w_o[j], swa_sinks[j])
            ms, ksm, vsm = _swa_sample(a_s, cache_swa_k[j], cache_swa_v[j], past_len,
                                       swa_w_qkv[j], swa_b_qkv[j], swa_w_o[j], swa_sinks[j])
            swa_kp.append(kp); swa_vp.append(vp); swa_ks.append(ksm); swa_vs.append(vsm)
        elif kind == 1:
            mp, sp, cp = _gdn_prompt(ap, gdn_w_in[j], gdn_conv_w[j], gdn_A_log[j], gdn_dt_bias[j],
                                     gdn_norm[j], gdn_w_out[j])
            ms, ss, cs = _gdn_sample(a_s, state_gdn_ssm[j], state_gdn_conv[j], gdn_w_in[j], gdn_conv_w[j],
                                     gdn_A_log[j], gdn_dt_bias[j], gdn_norm[j], gdn_w_out[j])
            gdn_sp.append(sp); gdn_cp.append(cp); gdn_ss.append(ss); gdn_cs.append(cs)
        else:
            mp, lp, rp = _mla_prompt(ap, mla_w_in[j], mla_q_norm[j], mla_w_uq[j], mla_kv_norm[j],
                                     mla_w_ukv[j], mla_w_o[j])
            lat_pages = cache_mla_latent[j, page_table]
            rope_pages = cache_mla_krope[j, page_table]
            ms, ls, rs = _mla_sample(a_s, lat_pages, rope_pages, past_len, mla_w_in[j], mla_q_norm[j],
                                     mla_w_uq[j], mla_kv_norm[j], mla_w_ukv[j], mla_w_o[j])
            mla_lp.append(lp); mla_rp.append(rp); mla_ls.append(ls); mla_rs.append(rs)
        hp = hp + _rms_norm(mp, norm_mix_post[layer])
        hs = hs + _rms_norm(ms, norm_mix_post[layer])
        hp = hp + _rms_norm(_ffn(_rms_norm(hp, norm_ffn_pre[layer]), ffn_w_up[layer], ffn_w_down[layer]), norm_ffn_post[layer])
        hs = hs + _rms_norm(_ffn(_rms_norm(hs, norm_ffn_pre[layer]), ffn_w_up[layer], ffn_w_down[layer]), norm_ffn_post[layer])
    return (hp, hs,
            jnp.stack(swa_kp), jnp.stack(swa_vp), jnp.stack(swa_ks), jnp.stack(swa_vs),
            jnp.stack(gdn_sp), jnp.stack(gdn_cp), jnp.stack(gdn_ss), jnp.stack(gdn_cs),
            jnp.stack(mla_lp), jnp.stack(mla_rp), jnp.stack(mla_ls), jnp.stack(mla_rs))
```

```python
import functools
import math

import jax
import jax.numpy as jnp
from jax import lax
from jax.experimental import pallas as pl
from jax.experimental.pallas import tpu as pltpu

F32 = jnp.float32
BF16 = jnp.bfloat16

RMS_EPS = 1e-6
ROPE_THETA = 10000.0
NEG_BIG = -1e30

LANES = 128
VMEM_LIMIT_BYTES = 56 * 1024 * 1024

SWA_HEAD_DIM = 64
SWA_GROUP = 8
SWA_WINDOW = 128
GDN_HEAD_DIM = 128
GDN_V_PER_QK = 2
GDN_CONV = 4
GDN_CHUNK = 64
MLA_HEADS = 16
MLA_Q_LORA = 512
MLA_KV_LORA = 512
MLA_NOPE = 128
MLA_ROPE = 64
MLA_V = 128


def _params(*sem):
    return pltpu.CompilerParams(dimension_semantics=sem, vmem_limit_bytes=VMEM_LIMIT_BYTES)


def _rms(y, g):
    return y * lax.rsqrt(jnp.mean(y * y, axis=-1, keepdims=True) + RMS_EPS) * g


def _dot(a, b):
    return jnp.dot(a, b, preferred_element_type=F32)


def _dot_nt(a, b):
    return lax.dot_general(a, b, (((1,), (1,)), ((), ())), preferred_element_type=F32)


def _dot_tn(a, b):
    return lax.dot_general(a, b, (((0,), (0,)), ((), ())), preferred_element_type=F32)


def _sigmoid(x):
    return 1.0 / (1.0 + jnp.exp(-x))


def _silu(x):
    return x * _sigmoid(x)


def _softplus(x):
    return jnp.maximum(x, 0.0) + jnp.log(1.0 + jnp.exp(-jnp.abs(x)))


def _rope_block(y, cos, sin_signed):
    lane = lax.broadcasted_iota(jnp.int32, (1, LANES), 1)
    first = (lane % 64) < 32
    swapped = jnp.where(first, pltpu.roll(y, 96, 1), pltpu.roll(y, 32, 1))
    return y * cos + swapped * sin_signed


def _rope_tables(pos):
    half = SWA_HEAD_DIM // 2
    inv_freq = ROPE_THETA ** (-jnp.arange(half, dtype=F32) * (2.0 / SWA_HEAD_DIM))
    ang = pos.astype(F32)[:, None] * inv_freq[None, :]
    cos, sin = jnp.cos(ang), jnp.sin(ang)
    return jnp.tile(cos, (1, 4)), jnp.tile(jnp.concatenate([-sin, sin], axis=1), (1, 2))


def _rmsnorm_kernel(x_ref, g_ref, o_ref):
    o_ref[...] = _rms(x_ref[...], g_ref[...]).astype(o_ref.dtype)


def _rmsnorm(x, g, tm):
    m, d = x.shape
    return pl.pallas_call(
        _rmsnorm_kernel,
        grid=(m // tm,),
        in_specs=[pl.BlockSpec((tm, d), lambda i: (i, 0)), pl.BlockSpec((1, d), lambda i: (0, 0))],
        out_specs=pl.BlockSpec((tm, d), lambda i: (i, 0)),
        out_shape=jax.ShapeDtypeStruct((m, d), BF16),
        compiler_params=_params("parallel"),
        name="rmsnorm",
    )(x, g.reshape(1, d))


def _mm_kernel(*refs, has_bias, rope_cols, tn):
    a_ref, w_ref = refs[0], refs[1]
    idx = 2
    if has_bias:
        b_ref = refs[idx]
        idx += 1
    if rope_cols is not None:
        cos_ref, sin_ref = refs[idx], refs[idx + 1]
        idx += 2
    o_ref = refs[idx]
    y = _dot(a_ref[...], w_ref[...])
    if has_bias:
        y = y + b_ref[...]
    if rope_cols is None:
        o_ref[...] = y.astype(o_ref.dtype)
        return
    j = pl.program_id(1)
    cos, sin = cos_ref[...], sin_ref[...]
    for c in range(tn // LANES):
        col0 = j * tn + c * LANES
        on = (col0 >= rope_cols[0]) & (col0 < rope_cols[1])
        yc = y[:, c * LANES:(c + 1) * LANES]
        o_ref[:, c * LANES:(c + 1) * LANES] = jnp.where(on, _rope_block(yc, cos, sin), yc).astype(o_ref.dtype)


def _matmul(a, w, tm, tn, bias=None, rope=None, rope_cols=None, out_dtype=BF16, name="matmul"):
    m, k = a.shape
    n = w.shape[1]
    assert m % tm == 0 and n % tn == 0, (m, tm, n, tn)
    in_specs = [pl.BlockSpec((tm, k), lambda i, j: (i, 0)), pl.BlockSpec((k, tn), lambda i, j: (0, j))]
    args = [a, w]
    if bias is not None:
        in_specs.append(pl.BlockSpec((1, tn), lambda i, j: (0, j)))
        args.append(bias.reshape(1, n).astype(F32))
    if rope is not None:
        in_specs += [pl.BlockSpec((tm, LANES), lambda i, j: (i, 0))] * 2
        args += list(rope)
    return pl.pallas_call(
        functools.partial(_mm_kernel, has_bias=bias is not None, rope_cols=rope_cols, tn=tn),
        grid=(m // tm, n // tn),
        in_specs=in_specs,
        out_specs=pl.BlockSpec((tm, tn), lambda i, j: (i, j)),
        out_shape=jax.ShapeDtypeStruct((m, n), out_dtype),
        compiler_params=_params("parallel", "arbitrary"),
        name=name,
    )(*args)


def _proj_res_kernel(*refs, has_up, has_next, nk):
    if has_up:
        x_ref, wu_ref, w_ref, h_ref, gp_ref = refs[:5]
        idx = 5
    else:
        x_ref, w_ref, h_ref, gp_ref = refs[:4]
        idx = 4
    if has_next:
        gn_ref = refs[idx]
        idx += 1
    ho_ref = refs[idx]
    idx += 1
    if has_next:
        ao_ref = refs[idx]
        idx += 1
    acc_ref = refs[idx] if nk > 1 else None

    if has_up:
        u = jnp.maximum(_dot(x_ref[...], wu_ref[...]), 0.0)
        part = _dot((u * u).astype(BF16), w_ref[...])
    else:
        part = _dot(x_ref[...], w_ref[...])

    def finish(y):
        hn = h_ref[...] + _rms(y, gp_ref[...])
        ho_ref[...] = hn
        if has_next:
            ao_ref[...] = _rms(hn, gn_ref[...]).astype(BF16)

    if nk == 1:
        finish(part)
        return
    k = pl.program_id(1)

    @pl.when(k == 0)
    def _():
        acc_ref[...] = part

    @pl.when(k > 0)
    def _():
        acc_ref[...] += part

    @pl.when(k == nk - 1)
    def _():
        finish(acc_ref[...])


def _proj_res(x, w, h, g_post, g_next, tm, tk, w_up=None, name="proj_res"):
    m, d = h.shape
    kdim = w.shape[0]
    nk = kdim // tk
    has_up, has_next = w_up is not None, g_next is not None
    if has_up:
        dk = x.shape[1]
        in_specs = [pl.BlockSpec((tm, dk), lambda i, k: (i, 0)),
                    pl.BlockSpec((dk, tk), lambda i, k: (0, k)),
                    pl.BlockSpec((tk, d), lambda i, k: (k, 0))]
        args = [x, w_up, w]
    else:
        in_specs = [pl.BlockSpec((tm, tk), lambda i, k: (i, k)),
                    pl.BlockSpec((tk, d), lambda i, k: (k, 0))]
        args = [x, w]
    row = pl.BlockSpec((tm, d), lambda i, k: (i, 0))
    vec = pl.BlockSpec((1, d), lambda i, k: (0, 0))
    in_specs += [row, vec]
    args += [h, g_post.reshape(1, d)]
    out_specs, out_shape = [row], [jax.ShapeDtypeStruct((m, d), F32)]
    if has_next:
        in_specs.append(vec)
        args.append(g_next.reshape(1, d))
        out_specs.append(row)
        out_shape.append(jax.ShapeDtypeStruct((m, d), BF16))
    outs = pl.pallas_call(
        functools.partial(_proj_res_kernel, has_up=has_up, has_next=has_next, nk=nk),
        grid=(m // tm, nk),
        in_specs=in_specs,
        out_specs=out_specs,
        out_shape=out_shape,
        scratch_shapes=[pltpu.VMEM((tm, d), F32)] if nk > 1 else [],
        compiler_params=_params("parallel", "arbitrary"),
        name=name,
    )(*args)
    return (outs[0], outs[1]) if has_next else (outs[0], None)


def _softmax_sink_unnorm(s, sink):
    m = jnp.maximum(jnp.max(s, axis=-1, keepdims=True), sink)
    p = jnp.exp(s - m)
    return p, 1.0 / (jnp.sum(p, axis=-1, keepdims=True) + jnp.exp(sink - m))


def _swa_prompt_kernel(sink_ref, q_ref, kp_ref, kc_ref, vp_ref, vc_ref, o_ref, *, nb):
    w = SWA_WINDOW
    n = pl.program_id(0) % nb
    kvh = pl.program_id(1)
    odd = (kvh % 2) == 1
    lane = lax.broadcasted_iota(jnp.int32, (1, LANES), 1)
    mine = (lane >= 64) == odd

    def pair_operand(p_ref, c_ref):
        blk = jnp.concatenate([p_ref[...], c_ref[...]], axis=0).astype(F32)
        own = jnp.where(mine, blk, 0.0)
        other = pltpu.roll(own, 64, 1)
        lo = jnp.where(odd, other, own)
        hi = jnp.where(odd, own, other)
        return jnp.concatenate([lo, hi], axis=0).astype(BF16)

    k2 = pair_operand(kp_ref, kc_ref)
    v2 = pair_operand(vp_ref, vc_ref)
    qi = lax.broadcasted_iota(jnp.int32, (w, 2 * w), 0)
    kj = lax.broadcasted_iota(jnp.int32, (w, 2 * w), 1)
    rel = qi + w - kj
    mask = (rel >= 0) & (rel < w) & ((n > 0) | (kj >= w))
    for p in range(SWA_GROUP // 2):
        s2 = _dot_nt(q_ref[:, p * LANES:(p + 1) * LANES], k2)
        ps, rs = [], []
        for e in range(2):
            s = jnp.where(mask, s2[:, e * 2 * w:(e + 1) * 2 * w], NEG_BIG)
            pe, re = _softmax_sink_unnorm(s, sink_ref[kvh * SWA_GROUP + 2 * p + e])
            ps.append(pe.astype(BF16))
            rs.append(re)
        o2 = _dot(jnp.concatenate(ps, axis=1), v2)
        o_ref[:, p * LANES:(p + 1) * LANES] = (o2 * jnp.where(lane < 64, rs[0], rs[1])).astype(o_ref.dtype)


def _swa_prompt_attn(qkv, sinks, batch, seq):
    w = SWA_WINDOW
    nb = seq // w
    n_heads = sinks.shape[0]
    kvh = n_heads // SWA_GROUP
    q_dim = n_heads * SWA_HEAD_DIM
    kblk0 = q_dim // LANES
    vblk0 = (q_dim + kvh * SWA_HEAD_DIM) // LANES

    def prev(i):
        return jnp.where(i % nb == 0, i, i - 1)

    return pl.pallas_call(
        functools.partial(_swa_prompt_kernel, nb=nb),
        grid=(batch * nb, kvh),
        in_specs=[
            pl.BlockSpec(memory_space=pltpu.SMEM),
            pl.BlockSpec((w, SWA_GROUP * SWA_HEAD_DIM), lambda i, h: (i, h)),
            pl.BlockSpec((w, LANES), lambda i, h: (prev(i), kblk0 + h // 2)),
            pl.BlockSpec((w, LANES), lambda i, h: (i, kblk0 + h // 2)),
            pl.BlockSpec((w, LANES), lambda i, h: (prev(i), vblk0 + h // 2)),
            pl.BlockSpec((w, LANES), lambda i, h: (i, vblk0 + h // 2)),
        ],
        out_specs=pl.BlockSpec((w, SWA_GROUP * SWA_HEAD_DIM), lambda i, h: (i, h)),
        out_shape=jax.ShapeDtypeStruct((batch * seq, q_dim), BF16),
        compiler_params=_params("parallel", "arbitrary"),
        name="swa_prompt_attn",
    )(sinks.astype(F32), qkv, qkv, qkv, qkv, qkv)


def _swa_sample_kernel(q_ref, kn_ref, vn_ref, ck_ref, cv_ref, sink_ref, o_ref, cko_ref, cvo_ref, *, bt, npair):
    p_rows = ck_ref.shape[1]
    lane = lax.broadcasted_iota(jnp.int32, (1, LANES), 1)
    row = lax.broadcasted_iota(jnp.int32, (p_rows, 1), 0)
    grp = lax.broadcasted_iota(jnp.int32, (2 * SWA_GROUP, 1), 0)
    for b in range(bt):
        k_new, v_new = kn_ref[b], vn_ref[b]
        cko_ref[b, pl.ds(0, p_rows - 1), :] = ck_ref[b, pl.ds(1, p_rows - 1), :]
        cvo_ref[b, pl.ds(0, p_rows - 1), :] = cv_ref[b, pl.ds(1, p_rows - 1), :]
        cko_ref[b, pl.ds(p_rows - 1, 1), :] = k_new
        cvo_ref[b, pl.ds(p_rows - 1, 1), :] = v_new
        for c in range(npair):
            sl = slice(c * LANES, (c + 1) * LANES)
            kk = jnp.where(row == 0, k_new[:, sl], ck_ref[b, :, sl]).astype(BF16)
            vv = jnp.where(row == 0, v_new[:, sl], cv_ref[b, :, sl]).astype(BF16)
            q2 = q_ref[b, pl.ds(c * 2 * SWA_GROUP, 2 * SWA_GROUP), :]
            q2 = jnp.where((lane >= 64) == (grp >= SWA_GROUP), q2, 0.0).astype(BF16)
            s = _dot_nt(q2, kk)
            pe, re = _softmax_sink_unnorm(s, sink_ref[c])
            o2 = _dot(pe.astype(BF16), vv) * re
            o_ref[b, pl.ds(c * SWA_GROUP, SWA_GROUP), :] = jnp.where(
                lane < 64, o2[:SWA_GROUP], o2[SWA_GROUP:]).astype(o_ref.dtype)


def _swa_sample_attn(qkv, ck, cv, sinks):
    bsz, p_rows, kvh, hd = ck.shape
    assert p_rows == SWA_WINDOW and hd == SWA_HEAD_DIM
    n_heads = kvh * SWA_GROUP
    q_dim, kv_dim = n_heads * hd, kvh * hd
    npair = kvh // 2
    bt = 8
    q = qkv[:, :q_dim].reshape(bsz, npair, 2, SWA_GROUP, hd).transpose(0, 1, 3, 2, 4)
    q = jnp.broadcast_to(q.reshape(bsz, npair, 1, SWA_GROUP, 2 * hd), (bsz, npair, 2, SWA_GROUP, 2 * hd))
    q = q.reshape(bsz, npair * 2 * SWA_GROUP, 2 * hd)
    k_new = qkv[:, q_dim:q_dim + kv_dim].reshape(bsz, 1, kv_dim)
    v_new = qkv[:, q_dim + kv_dim:].reshape(bsz, 1, kv_dim)
    sink = sinks.astype(F32).reshape(npair, 2 * SWA_GROUP, 1)
    o, cko, cvo = pl.pallas_call(
        functools.partial(_swa_sample_kernel, bt=bt, npair=npair),
        grid=(bsz // bt,),
        in_specs=[
            pl.BlockSpec((bt, npair * 2 * SWA_GROUP, LANES), lambda i: (i, 0, 0)),
            pl.BlockSpec((bt, 1, kv_dim), lambda i: (i, 0, 0)),
            pl.BlockSpec((bt, 1, kv_dim), lambda i: (i, 0, 0)),
            pl.BlockSpec((bt, p_rows, kv_dim), lambda i: (i, 0, 0)),
            pl.BlockSpec((bt, p_rows, kv_dim), lambda i: (i, 0, 0)),
            pl.BlockSpec((npair, 2 * SWA_GROUP, 1), lambda i: (0, 0, 0)),
        ],
        out_specs=[
            pl.BlockSpec((bt, npair * SWA_GROUP, LANES), lambda i: (i, 0, 0)),
            pl.BlockSpec((bt, p_rows, kv_dim), lambda i: (i, 0, 0)),
            pl.BlockSpec((bt, p_rows, kv_dim), lambda i: (i, 0, 0)),
        ],
        out_shape=[
            jax.ShapeDtypeStruct((bsz, npair * SWA_GROUP, LANES), BF16),
            jax.ShapeDtypeStruct((bsz, p_rows, kv_dim), F32),
            jax.ShapeDtypeStruct((bsz, p_rows, kv_dim), F32),
        ],
        compiler_params=_params("parallel"),
        name="swa_sample_attn",
    )(q, k_new, v_new, ck.reshape(bsz, p_rows, kv_dim), cv.reshape(bsz, p_rows, kv_dim), sink)
    o = o.reshape(bsz, npair, SWA_GROUP, 2, hd).transpose(0, 1, 3, 2, 4).reshape(bsz, q_dim)
    return o, cko.reshape(bsz, p_rows, kvh, hd), cvo.reshape(bsz, p_rows, kvh, hd)


def _gdn_gates_kernel(ba_ref, alog_ref, dt_ref, beta_ref, g_ref, gc_ref, *, chunk, nh):
    rows = ba_ref.shape[0]
    x = ba_ref[...]
    beta_ref[...] = _sigmoid(x)
    a = pltpu.roll(x, LANES - nh, 1)
    g = -jnp.exp(alog_ref[...]) * _softplus(a + dt_ref[...])
    g_ref[...] = g
    pos = lax.broadcasted_iota(jnp.int32, (rows, 1), 0) % chunk
    acc = g
    shift = 1
    while shift < chunk:
        acc = acc + jnp.where(pos >= shift, pltpu.roll(acc, shift, 0), 0.0)
        shift *= 2
    gc_ref[...] = acc


def _gdn_gates(ba, a_log, dt_bias, tr, chunk):
    m = ba.shape[0]
    nh = a_log.shape[0]
    row = pl.BlockSpec((tr, LANES), lambda i: (i, 0))
    vec = pl.BlockSpec((1, LANES), lambda i: (0, 0))

    def pad(x):
        return jnp.pad(x.astype(F32), (0, LANES - nh)).reshape(1, LANES)

    return pl.pallas_call(
        functools.partial(_gdn_gates_kernel, chunk=chunk, nh=nh),
        grid=(m // tr,),
        in_specs=[row, vec, vec],
        out_specs=[row, row, row],
        out_shape=[jax.ShapeDtypeStruct((m, LANES), F32)] * 3,
        compiler_params=_params("parallel"),
        name="gdn_gates",
    )(ba, pad(a_log), pad(dt_bias))


def _gdn_conv_kernel(x_ref, halo_ref, w_ref, o_ref, ext_ref, *, seq, key_dim, scale):
    tr, tc = x_ref.shape
    i, j = pl.program_id(0), pl.program_id(1)
    hb = halo_ref.shape[0]
    starts_seq = (i * tr) % seq == 0
    ext_ref[pl.ds(0, 8), :] = jnp.where(starts_seq, 0.0, halo_ref[pl.ds(hb - 8, 8), :].astype(F32))
    ext_ref[pl.ds(8, tr), :] = x_ref[...].astype(F32)
    acc = ext_ref[pl.ds(8 - (GDN_CONV - 1), tr), :] * w_ref[0:1, :]
    for t in range(1, GDN_CONV):
        acc = acc + ext_ref[pl.ds(8 - (GDN_CONV - 1) + t, tr), :] * w_ref[t:t + 1, :]
    c = _silu(acc)
    col0 = j * tc
    is_q = col0 < key_dim
    is_qk = col0 < 2 * key_dim
    for hblk in range(tc // LANES):
        ch = c[:, hblk * LANES:(hblk + 1) * LANES]
        nrm = ch * lax.rsqrt(jnp.sum(ch * ch, axis=-1, keepdims=True) + RMS_EPS)
        nrm = jnp.where(is_q, nrm * scale, nrm)
        o_ref[:, hblk * LANES:(hblk + 1) * LANES] = jnp.where(is_qk, nrm, ch).astype(o_ref.dtype)


def _gdn_conv(proj, conv_w, seq, key_dim, conv_dim, tr, tc):
    m = proj.shape[0]
    hb = 16
    return pl.pallas_call(
        functools.partial(_gdn_conv_kernel, seq=seq, key_dim=key_dim, scale=GDN_HEAD_DIM ** -0.5),
        grid=(m // tr, conv_dim // tc),
        in_specs=[
            pl.BlockSpec((tr, tc), lambda i, j: (i, j)),
            pl.BlockSpec((hb, tc), lambda i, j: (jnp.maximum(i * (tr // hb) - 1, 0), j)),
            pl.BlockSpec((GDN_CONV, tc), lambda i, j: (0, j)),
        ],
        out_specs=pl.BlockSpec((tr, tc), lambda i, j: (i, j)),
        out_shape=jax.ShapeDtypeStruct((m, conv_dim), BF16),
        scratch_shapes=[pltpu.VMEM((tr + 8, tc), F32)],
        compiler_params=_params("parallel", "arbitrary"),
        name="gdn_conv",
    )(proj, proj, conv_w.astype(F32))


GDN_INV_BASE = 16


def _unit_lower_inverse(a):
    c = a.shape[0]
    base = GDN_INV_BASE
    ri = lax.broadcasted_iota(jnp.int32, (c, c), 0)
    ci = lax.broadcasted_iota(jnp.int32, (c, c), 1)
    x = (ri == ci).astype(F32)
    diag_blocks = jnp.where(ri // base == ci // base, a, 0.0)
    col_in_block = ci % base
    for j in range(base - 1):
        coef = jnp.sum(jnp.where(col_in_block == j, diag_blocks, 0.0), axis=1, keepdims=True)
        rows = jnp.concatenate(
            [jnp.broadcast_to(x[b * base + j:b * base + j + 1, :], (base, c)) for b in range(c // base)], axis=0)
        x = x - coef * rows
    size = base
    while size < c:
        off = jnp.where((ri // (2 * size) == ci // (2 * size)) & (ri // size != ci // size), a, 0.0)
        xb = x.astype(BF16)
        x = x - _dot(_dot(xb, off.astype(BF16)).astype(BF16), xb)
        size *= 2
    return x


def _gdn_scan_kernel(q_ref, k_ref, v_ref, z_ref, gc_ref, beta_ref, gt_ref, nw_ref, y_ref, so_ref, s_ref, *, chunk):
    rows = q_ref.shape[0]
    hd = GDN_HEAD_DIM
    hq, t = pl.program_id(1), pl.program_id(2)

    @pl.when(t == 0)
    def _():
        s_ref[...] = jnp.zeros_like(s_ref)

    lane = lax.broadcasted_iota(jnp.int32, (1, LANES), 1)
    ri = lax.broadcasted_iota(jnp.int32, (chunk, chunk), 0)
    ci = lax.broadcasted_iota(jnp.int32, (chunk, chunk), 1)
    tri, strict = ri >= ci, ri > ci
    grow_all = [gt_ref[pl.ds(hq * GDN_V_PER_QK + e, 1), :] for e in range(GDN_V_PER_QK)]
    for c in range(rows // chunk):
        r0 = c * chunk
        q = q_ref[pl.ds(r0, chunk), :]
        k = k_ref[pl.ds(r0, chunk), :]
        gkk = _dot_nt(k, k)
        gqk = _dot_nt(q, k)
        for e in range(GDN_V_PER_QK):
            hv = hq * GDN_V_PER_QK + e
            pick = lane == hv
            gcol = jnp.sum(jnp.where(pick, gc_ref[pl.ds(r0, chunk), :], 0.0), axis=1, keepdims=True)
            bcol = jnp.sum(jnp.where(pick, beta_ref[pl.ds(r0, chunk), :], 0.0), axis=1, keepdims=True)
            grow = grow_all[e][:, r0:r0 + chunk]
            decay = jnp.where(tri, jnp.exp(jnp.where(tri, gcol - grow, 0.0)), 0.0)
            tinv = _unit_lower_inverse(jnp.where(strict, bcol * gkk * decay, 0.0))
            state = s_ref[e]
            sb = state.astype(BF16)
            eg = jnp.exp(gcol)
            rhs = bcol * (v_ref[pl.ds(r0, chunk), pl.ds(e * hd, hd)].astype(F32) - eg * _dot(k, sb))
            v_new = _dot(tinv.astype(BF16), rhs.astype(BF16))
            o = eg * _dot(q, sb) + _dot((gqk * decay).astype(BF16), v_new.astype(BF16))
            g_last = gcol[chunk - 1:chunk, :]
            s_ref[e] = jnp.exp(g_last) * state + _dot_tn(k, (jnp.exp(g_last - gcol) * v_new).astype(BF16))
            gate = _silu(z_ref[pl.ds(r0, chunk), pl.ds(e * hd, hd)].astype(F32))
            y_ref[pl.ds(r0, chunk), pl.ds(e * hd, hd)] = (_rms(o, nw_ref[...]) * gate).astype(y_ref.dtype)

    @pl.when(t == pl.num_programs(2) - 1)
    def _():
        so_ref[0] = s_ref[...]


def _gdn_scan(qkv, proj, gc, beta, gc_t, norm_w, batch, seq, n_qk, rows, chunk):
    hd = GDN_HEAD_DIM
    m = batch * seq
    nt = seq // rows
    n_v = n_qk * GDN_V_PER_QK
    vw = GDN_V_PER_QK * hd
    k_blk0 = n_qk
    v_blk0 = 2 * n_qk * hd // vw
    z_blk0 = (2 * n_qk * hd + n_v * hd) // vw
    y, s_fin = pl.pallas_call(
        functools.partial(_gdn_scan_kernel, chunk=chunk),
        grid=(batch, n_qk, nt),
        in_specs=[
            pl.BlockSpec((rows, hd), lambda b, h, t: (b * nt + t, h)),
            pl.BlockSpec((rows, hd), lambda b, h, t: (b * nt + t, k_blk0 + h)),
            pl.BlockSpec((rows, vw), lambda b, h, t: (b * nt + t, v_blk0 + h)),
            pl.BlockSpec((rows, vw), lambda b, h, t: (b * nt + t, z_blk0 + h)),
            pl.BlockSpec((rows, LANES), lambda b, h, t: (b * nt + t, 0)),
            pl.BlockSpec((rows, LANES), lambda b, h, t: (b * nt + t, 0)),
            pl.BlockSpec((n_v, rows), lambda b, h, t: (0, b * nt + t)),
            pl.BlockSpec((1, hd), lambda b, h, t: (0, 0)),
        ],
        out_specs=[
            pl.BlockSpec((rows, vw), lambda b, h, t: (b * nt + t, h)),
            pl.BlockSpec((1, GDN_V_PER_QK, hd, hd), lambda b, h, t: (b, h, 0, 0)),
        ],
        out_shape=[
            jax.ShapeDtypeStruct((m, n_v * hd), BF16),
            jax.ShapeDtypeStruct((batch, n_v, hd, hd), F32),
        ],
        scratch_shapes=[pltpu.VMEM((GDN_V_PER_QK, hd, hd), F32)],
        compiler_params=_params("parallel", "parallel", "arbitrary"),
        name="gdn_scan",
    )(qkv, qkv, qkv, proj, gc, beta, gc_t, norm_w.reshape(1, hd).astype(F32))
    return y, s_fin


def _gdn_conv_step_kernel(s0_ref, s1_ref, s2_ref, x_ref, w_ref, o_ref, *, key_dim, scale):
    tc = x_ref.shape[1]
    acc = s0_ref[...] * w_ref[0:1, :] + s1_ref[...] * w_ref[1:2, :] + s2_ref[...] * w_ref[2:3, :] \
        + x_ref[...] * w_ref[3:4, :]
    c = _silu(acc)
    col0 = pl.program_id(0) * tc
    is_q = col0 < key_dim
    is_qk = col0 < 2 * key_dim
    for hblk in range(tc // LANES):
        ch = c[:, hblk * LANES:(hblk + 1) * LANES]
        nrm = ch * lax.rsqrt(jnp.sum(ch * ch, axis=-1, keepdims=True) + RMS_EPS)
        nrm = jnp.where(is_q, nrm * scale, nrm)
        o_ref[:, hblk * LANES:(hblk + 1) * LANES] = jnp.where(is_qk, nrm, ch)


def _gdn_conv_step(conv_state, mixed, conv_w, key_dim, tc):
    bsz, taps, cdim = conv_state.shape
    assert taps == GDN_CONV - 1
    ncol = cdim // tc
    flat = conv_state.reshape(bsz, taps * cdim)
    return pl.pallas_call(
        functools.partial(_gdn_conv_step_kernel, key_dim=key_dim, scale=GDN_HEAD_DIM ** -0.5),
        grid=(ncol,),
        in_specs=[
            pl.BlockSpec((bsz, tc), lambda j: (0, j)),
            pl.BlockSpec((bsz, tc), lambda j: (0, ncol + j)),
            pl.BlockSpec((bsz, tc), lambda j: (0, 2 * ncol + j)),
            pl.BlockSpec((bsz, tc), lambda j: (0, j)),
            pl.BlockSpec((GDN_CONV, tc), lambda j: (0, j)),
        ],
        out_specs=pl.BlockSpec((bsz, tc), lambda j: (0, j)),
        out_shape=jax.ShapeDtypeStruct((bsz, cdim), F32),
        compiler_params=_params("parallel"),
        name="gdn_conv_step",
    )(flat, flat, flat, mixed, conv_w.astype(F32))


def _gdn_step_kernel(q_ref, k_ref, v_ref, z_ref, g_ref, beta_ref, nw_ref, s_ref, y_ref, so_ref, o_scr):
    bsz = q_ref.shape[0]
    hv = pl.program_id(0)
    pick = lax.broadcasted_iota(jnp.int32, (1, LANES), 1) == hv
    egcol = jnp.exp(jnp.sum(jnp.where(pick, g_ref[...], 0.0), axis=1, keepdims=True))
    bcol = jnp.sum(jnp.where(pick, beta_ref[...], 0.0), axis=1, keepdims=True)
    q_t = q_ref[...].T
    k_t = k_ref[...].T
    for b in range(bsz):
        kc, qc = k_t[:, b:b + 1], q_t[:, b:b + 1]
        s1 = s_ref[b, 0] * egcol[b:b + 1, :]
        delta = (v_ref[b:b + 1, :] - jnp.sum(s1 * kc, axis=0, keepdims=True)) * bcol[b:b + 1, :]
        s2 = s1 + kc * delta
        so_ref[b, 0] = s2
        o_scr[b:b + 1, :] = jnp.sum(s2 * qc, axis=0, keepdims=True)
    y_ref[...] = (_rms(o_scr[...], nw_ref[...]) * _silu(z_ref[...])).astype(y_ref.dtype)


def _gdn_step(qkv, proj, g, beta, norm_w, ssm, n_qk):
    bsz, n_v, hd, _ = ssm.shape
    k_blk0, v_blk0 = n_qk, 2 * n_qk
    z_blk0 = 2 * n_qk + n_v
    row = pl.BlockSpec((bsz, LANES), lambda h: (0, 0))
    st = pl.BlockSpec((bsz, 1, hd, hd), lambda h: (0, h, 0, 0))
    return pl.pallas_call(
        _gdn_step_kernel,
        grid=(n_v,),
        in_specs=[
            pl.BlockSpec((bsz, hd), lambda h: (0, h // GDN_V_PER_QK)),
            pl.BlockSpec((bsz, hd), lambda h: (0, k_blk0 + h // GDN_V_PER_QK)),
            pl.BlockSpec((bsz, hd), lambda h: (0, v_blk0 + h)),
            pl.BlockSpec((bsz, hd), lambda h: (0, z_blk0 + h)),
            row, row,
            pl.BlockSpec((1, hd), lambda h: (0, 0)),
            st,
        ],
        out_specs=[pl.BlockSpec((bsz, hd), lambda h: (0, h)), st],
        out_shape=[jax.ShapeDtypeStruct((bsz, n_v * hd), BF16), jax.ShapeDtypeStruct(ssm.shape, F32)],
        scratch_shapes=[pltpu.VMEM((bsz, hd), F32)],
        compiler_params=_params("parallel"),
        name="gdn_step",
    )(qkv, qkv, qkv, proj, g, beta, norm_w.reshape(1, hd).astype(F32), ssm)


MLA_IN_COLS = MLA_Q_LORA + MLA_KV_LORA + 2 * LANES
MLA_Q_COLS = MLA_HEADS * (MLA_NOPE + MLA_ROPE)


def _mla_proj_kernel(*refs, has_kv):
    a_ref, win_ref, qn_ref, kvn_ref, wuq_ref = refs[:5]
    idx = 5
    if has_kv:
        wukv_ref = refs[idx]
        idx += 1
    cos_ref, sin_ref, q_ref = refs[idx:idx + 3]
    idx += 3
    if has_kv:
        kv_ref = refs[idx]
        idx += 1
    ckv_ref, kr_ref = refs[idx:idx + 2]
    cos, sin = cos_ref[...], sin_ref[...]
    p = _dot(a_ref[...], win_ref[...])
    cq = _rms(p[:, :MLA_Q_LORA], qn_ref[...]).astype(BF16)
    ckv = _rms(p[:, MLA_Q_LORA:MLA_Q_LORA + MLA_KV_LORA], kvn_ref[...])
    ckv_ref[...] = ckv
    r0 = MLA_Q_LORA + MLA_KV_LORA
    for c in range(2):
        kr_ref[:, c * LANES:(c + 1) * LANES] = _rope_block(p[:, r0 + c * LANES:r0 + (c + 1) * LANES], cos, sin)
    q = _dot(cq, wuq_ref[...])
    nope = MLA_HEADS * MLA_NOPE
    q_ref[:, :nope] = q[:, :nope].astype(q_ref.dtype)
    for c in range(MLA_HEADS * MLA_ROPE // LANES):
        sl = slice(nope + c * LANES, nope + (c + 1) * LANES)
        q_ref[:, sl] = _rope_block(q[:, sl], cos, sin).astype(q_ref.dtype)
    if has_kv:
        kv_ref[...] = _dot(ckv.astype(BF16), wukv_ref[...]).astype(kv_ref.dtype)


def _mla_project(a, w_in, q_norm, w_uq, kv_norm, rope, tm, w_ukv=None):
    m, d = a.shape
    has_kv = w_ukv is not None
    row = lambda n: pl.BlockSpec((tm, n), lambda i: (i, 0))
    full = lambda s: pl.BlockSpec(s, lambda i: (0, 0))
    in_specs = [row(d), full(w_in.shape), full((1, MLA_Q_LORA)), full((1, MLA_KV_LORA)), full(w_uq.shape)]
    args = [a, w_in, q_norm.reshape(1, -1).astype(F32), kv_norm.reshape(1, -1).astype(F32), w_uq]
    out_specs, out_shape = [row(MLA_Q_COLS)], [jax.ShapeDtypeStruct((m, MLA_Q_COLS), BF16)]
    if has_kv:
        in_specs.append(full(w_ukv.shape))
        args.append(w_ukv)
        out_specs.append(row(w_ukv.shape[1]))
        out_shape.append(jax.ShapeDtypeStruct((m, w_ukv.shape[1]), BF16))
    in_specs += [row(LANES), row(LANES)]
    args += list(rope)
    out_specs += [row(MLA_KV_LORA), row(2 * LANES)]
    out_shape += [jax.ShapeDtypeStruct((m, MLA_KV_LORA), F32), jax.ShapeDtypeStruct((m, 2 * LANES), F32)]
    return pl.pallas_call(
        functools.partial(_mla_proj_kernel, has_kv=has_kv),
        grid=(m // tm,),
        in_specs=in_specs,
        out_specs=out_specs,
        out_shape=out_shape,
        compiler_params=_params("parallel"),
        name="mla_project",
    )(*args)


def _mla_attn_kernel(qn_ref, qr_ref, kn_ref, kr_ref, v_ref, o_ref, m_ref, l_ref, acc_ref, *, tq, tk):
    qi, kj = pl.program_id(2), pl.program_id(3)
    last = ((qi + 1) * tq - 1) // tk

    @pl.when(kj == 0)
    def _():
        m_ref[...] = jnp.full_like(m_ref, NEG_BIG)
        l_ref[...] = jnp.zeros_like(l_ref)
        acc_ref[...] = jnp.zeros_like(acc_ref)

    @pl.when(kj <= last)
    def _():
        s = _dot_nt(qn_ref[...], kn_ref[...]) + _dot_nt(qr_ref[...], kr_ref[...].astype(BF16))
        qpos = qi * tq + lax.broadcasted_iota(jnp.int32, (tq, tk), 0)
        kpos = kj * tk + lax.broadcasted_iota(jnp.int32, (tq, tk), 1)
        s = jnp.where(kpos <= qpos, s, NEG_BIG)
        m_prev = m_ref[...]
        m_new = jnp.maximum(m_prev, jnp.max(s, axis=-1, keepdims=True))
        alpha = jnp.exp(m_prev - m_new)
        p = jnp.exp(s - m_new)
        l_ref[...] = alpha * l_ref[...] + jnp.sum(p, axis=-1, keepdims=True)
        acc_ref[...] = alpha * acc_ref[...] + _dot(p.astype(BF16), v_ref[...])
        m_ref[...] = m_new

    @pl.when(kj == last)
    def _():
        o_ref[...] = (acc_ref[...] / l_ref[...]).astype(o_ref.dtype)


def _mla_prompt_attn(q, kv, kr, batch, seq, tq, tk):
    nq, nk = seq // tq, seq // tk
    nope_blks = MLA_HEADS * MLA_NOPE // LANES

    def kv_row(b, qi, kj):
        return b * nk + jnp.minimum(kj, ((qi + 1) * tq - 1) // tk)

    return pl.pallas_call(
        functools.partial(_mla_attn_kernel, tq=tq, tk=tk),
        grid=(batch, MLA_HEADS, nq, nk),
        in_specs=[
            pl.BlockSpec((tq, LANES), lambda b, h, qi, kj: (b * nq + qi, h)),
            pl.BlockSpec((tq, LANES), lambda b, h, qi, kj: (b * nq + qi, nope_blks + h // 2)),
            pl.BlockSpec((tk, LANES), lambda b, h, qi, kj: (kv_row(b, qi, kj), 2 * h)),
            pl.BlockSpec((tk, LANES), lambda b, h, qi, kj: (kv_row(b, qi, kj), h % 2)),
            pl.BlockSpec((tk, LANES), lambda b, h, qi, kj: (kv_row(b, qi, kj), 2 * h + 1)),
        ],
        out_specs=pl.BlockSpec((tq, LANES), lambda b, h, qi, kj: (b * nq + qi, h)),
        out_shape=jax.ShapeDtypeStruct((batch * seq, MLA_HEADS * MLA_V), BF16),
        scratch_shapes=[pltpu.VMEM((tq, 1), F32), pltpu.VMEM((tq, 1), F32), pltpu.VMEM((tq, MLA_V), F32)],
        compiler_params=_params("parallel", "parallel", "parallel", "arbitrary"),
        name="mla_prompt_attn",
    )(q, q, kv, kr, kv)


def _head_mm_kernel(x_ref, w_ref, o_ref, *, transpose_w):
    y = _dot_nt(x_ref[...], w_ref[...]) if transpose_w else _dot(x_ref[...], w_ref[...])
    o_ref[...] = y.astype(o_ref.dtype)


def _head_matmul(x, w_ukv, transpose_w):
    bsz = x.shape[0]
    xw = x.shape[1] // MLA_HEADS
    ow = MLA_KV_LORA if transpose_w else MLA_V
    return pl.pallas_call(
        functools.partial(_head_mm_kernel, transpose_w=transpose_w),
        grid=(MLA_HEADS,),
        in_specs=[
            pl.BlockSpec((bsz, xw), lambda h: (0, h)),
            pl.BlockSpec((MLA_KV_LORA, LANES), lambda h: (0, 2 * h if transpose_w else 2 * h + 1)),
        ],
        out_specs=pl.BlockSpec((bsz, ow), lambda h: (0, h)),
        out_shape=jax.ShapeDtypeStruct((bsz, MLA_HEADS * ow), BF16),
        compiler_params=_params("parallel"),
        name="mla_head_matmul",
    )(x, w_ukv)


def _mla_decode_kernel(pt_ref, ql_ref, qr_ref, ckv_ref, krn_ref, *rest, pages):
    lat_refs, rope_refs = rest[:pages], rest[pages:2 * pages]
    o_ref, m_ref, l_ref, acc_ref = rest[2 * pages:]
    c = pl.program_id(1)

    @pl.when(c == 0)
    def _():
        m_ref[...] = jnp.full_like(m_ref, NEG_BIG)
        l_ref[...] = jnp.zeros_like(l_ref)
        acc_ref[...] = jnp.zeros_like(acc_ref)

    ql, qr = ql_ref[0], qr_ref[0]
    lat = jnp.concatenate([r[...].astype(BF16) for r in lat_refs], axis=0)
    rope = jnp.concatenate([r[...].astype(BF16) for r in rope_refs], axis=0)
    s = _dot_nt(ql, lat) + _dot_nt(qr, rope)
    m_prev = m_ref[...]
    m_new = jnp.maximum(m_prev, jnp.max(s, axis=-1, keepdims=True))
    alpha = jnp.exp(m_prev - m_new)
    p = jnp.exp(s - m_new)
    l_ref[...] = alpha * l_ref[...] + jnp.sum(p, axis=-1, keepdims=True)
    acc_ref[...] = alpha * acc_ref[...] + _dot(p.astype(BF16), lat)
    m_ref[...] = m_new

    @pl.when(c == pl.num_programs(1) - 1)
    def _():
        ckv, krn = ckv_ref[0], krn_ref[0]
        s_new = (jnp.sum(ql.astype(F32) * ckv, axis=-1, keepdims=True)
                 + jnp.sum(qr.astype(F32) * krn, axis=-1, keepdims=True))
        m_old = m_ref[...]
        m_fin = jnp.maximum(m_old, s_new)
        a_fin = jnp.exp(m_old - m_fin)
        p_new = jnp.exp(s_new - m_fin)
        l_fin = a_fin * l_ref[...] + p_new
        o_ref[0] = ((a_fin * acc_ref[...] + p_new * ckv) / l_fin).astype(o_ref.dtype)


def _mla_decode(q_lat, q_rope, ckv_new, kr_new, lat_cache, rope_cache, page_table, layer, pages):
    bsz, n_pages = page_table.shape
    page = lat_cache.shape[2]
    nh = q_lat.shape[1]

    def lat_spec(g):
        return pl.BlockSpec((None, None, page, MLA_KV_LORA), lambda b, c, pt: (layer, pt[b, c * pages + g], 0, 0))

    def rope_spec(g):
        return pl.BlockSpec((None, None, page, MLA_ROPE), lambda b, c, pt: (layer, pt[b, c * pages + g], 0, 0))

    grid_spec = pltpu.PrefetchScalarGridSpec(
        num_scalar_prefetch=1,
        grid=(bsz, n_pages // pages),
        in_specs=[
            pl.BlockSpec((1, nh, MLA_KV_LORA), lambda b, c, pt: (b, 0, 0)),
            pl.BlockSpec((1, nh, MLA_ROPE), lambda b, c, pt: (b, 0, 0)),
            pl.BlockSpec((1, 1, MLA_KV_LORA), lambda b, c, pt: (b, 0, 0)),
            pl.BlockSpec((1, 1, MLA_ROPE), lambda b, c, pt: (b, 0, 0)),
        ] + [lat_spec(g) for g in range(pages)] + [rope_spec(g) for g in range(pages)],
        out_specs=pl.BlockSpec((1, nh, MLA_KV_LORA), lambda b, c, pt: (b, 0, 0)),
        scratch_shapes=[pltpu.VMEM((nh, 1), F32), pltpu.VMEM((nh, 1), F32), pltpu.VMEM((nh, MLA_KV_LORA), F32)],
    )
    return pl.pallas_call(
        functools.partial(_mla_decode_kernel, pages=pages),
        grid_spec=grid_spec,
        out_shape=jax.ShapeDtypeStruct((bsz, nh, MLA_KV_LORA), BF16),
        compiler_params=_params("parallel", "arbitrary"),
        name="mla_decode",
    )(page_table, q_lat, q_rope, ckv_new, kr_new, *([lat_cache] * pages), *([rope_cache] * pages))


PROMPT_TM = 512
FFN_TK = 512
OUT_TK = 1024
GDN_ROWS = 256
MLA_TQ = 512
MLA_TK = 512
MLA_PAGES_PER_STEP = 8


def _swa_weights(w_qkv, b_qkv, n_heads):
    q_dim = n_heads * SWA_HEAD_DIM
    col_scale = jnp.where(jnp.arange(w_qkv.shape[1]) < q_dim, SWA_HEAD_DIM ** -0.5, 1.0).astype(F32)
    return (w_qkv * col_scale).astype(BF16), b_qkv * col_scale


def _swa_layer(a_p, a_s, rope_p, rope_s, w_qkv, b_qkv, sinks, ck, cv, batch, seq):
    n_heads = sinks.shape[0]
    kvh = ck.shape[2]
    q_dim, kv_dim = n_heads * SWA_HEAD_DIM, kvh * SWA_HEAD_DIM
    w, b = _swa_weights(w_qkv, b_qkv, n_heads)
    rope_cols = (0, q_dim + kv_dim)
    qkv_p = _matmul(a_p, w, PROMPT_TM, 512, bias=b, rope=rope_p, rope_cols=rope_cols, name="swa_qkv")
    o_p = _swa_prompt_attn(qkv_p, sinks, batch, seq)
    rows = min(SWA_WINDOW, seq)
    kv_tail = qkv_p.reshape(batch, seq, -1)[:, seq - rows:, q_dim:].astype(F32)
    k_p = kv_tail[..., :kv_dim].reshape(batch, rows, kvh, SWA_HEAD_DIM)
    v_p = kv_tail[..., kv_dim:].reshape(batch, rows, kvh, SWA_HEAD_DIM)
    qkv_s = _matmul(a_s, w, a_s.shape[0], 512, bias=b, rope=rope_s, rope_cols=rope_cols, out_dtype=F32,
                    name="swa_qkv")
    o_s, k_s, v_s = _swa_sample_attn(qkv_s, ck, cv, sinks)
    return o_p, o_s, (k_p, v_p, k_s, v_s)


def _gdn_layer(a_p, a_s, w_in, conv_w, a_log, dt_bias, norm_w, ssm, conv_state, batch, seq):
    n_v = a_log.shape[0]
    n_qk = n_v // GDN_V_PER_QK
    key_dim = n_qk * GDN_HEAD_DIM
    conv_dim = 2 * key_dim + n_v * GDN_HEAD_DIM
    main_cols = conv_dim + n_v * GDN_HEAD_DIM
    w_main = w_in[:, :main_cols].astype(BF16)
    w_ba = jnp.pad(w_in[:, main_cols:], ((0, 0), (0, LANES - 2 * n_v))).astype(BF16)
    proj_p = _matmul(a_p, w_main, PROMPT_TM, 1024, name="gdn_in")
    ba_p = _matmul(a_p, w_ba, PROMPT_TM, LANES, out_dtype=F32, name="gdn_in_gates")
    beta_p, _, gc_p = _gdn_gates(ba_p, a_log, dt_bias, GDN_ROWS, GDN_CHUNK)
    qkv_p = _gdn_conv(proj_p, conv_w, seq, key_dim, conv_dim, GDN_ROWS, 1024)
    y_p, s_p = _gdn_scan(qkv_p, proj_p, gc_p, beta_p, gc_p[:, :n_v].T, norm_w, batch, seq, n_qk, GDN_ROWS, GDN_CHUNK)
    taps = GDN_CONV - 1
    c_p = proj_p.reshape(batch, seq, -1)[:, seq - taps:, :conv_dim].astype(F32)
    bsz = a_s.shape[0]
    proj_s = _matmul(a_s, w_main, bsz, 1024, out_dtype=F32, name="gdn_in")
    ba_s = _matmul(a_s, w_ba, bsz, LANES, out_dtype=F32, name="gdn_in_gates")
    beta_s, g_s, _ = _gdn_gates(ba_s, a_log, dt_bias, bsz, 1)
    qkv_s = _gdn_conv_step(conv_state, proj_s, conv_w, key_dim, 1024)
    y_s, s_s = _gdn_step(qkv_s, proj_s, g_s, beta_s, norm_w, ssm, n_qk)
    c_s = jnp.concatenate([conv_state[:, 1:], proj_s[:, None, :conv_dim]], axis=1)
    return y_p, y_s, (s_p, c_p, s_s, c_s)


def _mla_weights(w_in, w_uq):
    zeros = jnp.zeros((w_in.shape[0], MLA_ROPE), w_in.dtype)
    kr = w_in[:, MLA_Q_LORA + MLA_KV_LORA:]
    w_in_p = jnp.concatenate([w_in[:, :MLA_Q_LORA + MLA_KV_LORA], kr, zeros, zeros, kr], axis=1)
    wq = w_uq.reshape(MLA_Q_LORA, MLA_HEADS, MLA_NOPE + MLA_ROPE) * (MLA_NOPE + MLA_ROPE) ** -0.5
    wq = jnp.concatenate([wq[..., :MLA_NOPE].reshape(MLA_Q_LORA, -1), wq[..., MLA_NOPE:].reshape(MLA_Q_LORA, -1)],
                         axis=1)
    return w_in_p.astype(BF16), wq.astype(BF16)


def _mla_layer(a_p, a_s, rope_p, rope_s, w_in, q_norm, w_uq, kv_norm, w_ukv, lat_cache, rope_cache, page_table,
               layer, batch, seq):
    w_in_p, wq = _mla_weights(w_in, w_uq)
    w_ukv = w_ukv.astype(BF16)
    nope = MLA_HEADS * MLA_NOPE
    q_p, kv_p, ckv_p, kr_p = _mla_project(a_p, w_in_p, q_norm, wq, kv_norm, rope_p, 256, w_ukv=w_ukv)
    o_p = _mla_prompt_attn(q_p, kv_p, kr_p, batch, seq, min(MLA_TQ, seq), min(MLA_TK, seq))
    bsz = a_s.shape[0]
    q_s, ckv_s, kr_s = _mla_project(a_s, w_in_p, q_norm, wq, kv_norm, rope_s, bsz)
    q_lat = _head_matmul(q_s[:, :nope], w_ukv, True).reshape(bsz, MLA_HEADS, MLA_KV_LORA)
    q_rope = q_s[:, nope:].reshape(bsz, MLA_HEADS, MLA_ROPE)
    kr_new = kr_s[:, :MLA_ROPE]
    o_lat = _mla_decode(q_lat, q_rope, ckv_s.reshape(bsz, 1, -1), kr_new.reshape(bsz, 1, -1), lat_cache, rope_cache,
                        page_table, layer, MLA_PAGES_PER_STEP)
    o_s = _head_matmul(o_lat.reshape(bsz, -1), w_ukv, False)
    outs = (ckv_p.reshape(batch, seq, -1), kr_p[:, :MLA_ROPE].reshape(batch, seq, -1),
            ckv_s.reshape(bsz, 1, -1), kr_new.reshape(bsz, 1, -1))
    return o_p, o_s, outs


def kernel(x_prompt, x_sample, cache_swa_k, cache_swa_v, state_gdn_ssm, state_gdn_conv, cache_mla_latent, cache_mla_krope, page_table, norm_mix_pre, norm_mix_post, norm_ffn_pre, norm_ffn_post, ffn_w_up, ffn_w_down, swa_w_qkv, swa_b_qkv, swa_w_o, swa_sinks, gdn_w_in, gdn_conv_w, gdn_A_log, gdn_dt_bias, gdn_norm, gdn_w_out, mla_w_in, mla_q_norm, mla_w_uq, mla_kv_norm, mla_w_ukv, mla_w_o):
    batch, seq, d_model = x_prompt.shape
    bsz, dec_seq, _ = x_sample.shape
    assert dec_seq == 1
    depth = norm_mix_pre.shape[0]
    past_len = page_table.shape[1] * cache_mla_latent.shape[2]
    hp = x_prompt.reshape(batch * seq, d_model)
    hs = x_sample.reshape(bsz, d_model)
    rope_p = _rope_tables(jnp.tile(jnp.arange(seq, dtype=jnp.int32), batch))
    rope_s = _rope_tables(jnp.full((bsz,), past_len, jnp.int32))
    a_p = _rmsnorm(hp, norm_mix_pre[0], PROMPT_TM)
    a_s = _rmsnorm(hs, norm_mix_pre[0], bsz)
    collected = ([], [], [])
    for layer in range(depth):
        kind, j = layer % 3, layer // 3
        if kind == 0:
            x_p, x_s, outs = _swa_layer(a_p, a_s, rope_p, rope_s, swa_w_qkv[j], swa_b_qkv[j], swa_sinks[j],
                                        cache_swa_k[j], cache_swa_v[j], batch, seq)
            w_o = swa_w_o[j]
        elif kind == 1:
            x_p, x_s, outs = _gdn_layer(a_p, a_s, gdn_w_in[j], gdn_conv_w[j], gdn_A_log[j], gdn_dt_bias[j],
                                        gdn_norm[j], state_gdn_ssm[j], state_gdn_conv[j], batch, seq)
            w_o = gdn_w_out[j]
        else:
            x_p, x_s, outs = _mla_layer(a_p, a_s, rope_p, rope_s, mla_w_in[j], mla_q_norm[j], mla_w_uq[j],
                                        mla_kv_norm[j], mla_w_ukv[j], cache_mla_latent, cache_mla_krope,
                                        page_table, j, batch, seq)
            w_o = mla_w_o[j]
        collected[kind].append(outs)
        w_o = w_o.astype(BF16)
        w_up, w_down = ffn_w_up[layer].astype(BF16), ffn_w_down[layer].astype(BF16)
        g_next = norm_mix_pre[layer + 1] if layer + 1 < depth else None
        hp, a_p = _proj_res(x_p, w_o, hp, norm_mix_post[layer], norm_ffn_pre[layer], PROMPT_TM, OUT_TK,
                            name="mixer_out")
        hs, a_s = _proj_res(x_s, w_o, hs, norm_mix_post[layer], norm_ffn_pre[layer], bsz, OUT_TK,
                            name="mixer_out")
        hp, a_p = _proj_res(a_p, w_down, hp, norm_ffn_post[layer], g_next, PROMPT_TM, FFN_TK, w_up=w_up,
                            name="ffn")
        hs, a_s = _proj_res(a_s, w_down, hs, norm_ffn_post[layer], g_next, bsz, FFN_TK, w_up=w_up, name="ffn")

    def stacked(kind):
        return tuple(jnp.stack(parts) for parts in zip(*collected[kind]))

    return ((hp.reshape(batch, seq, d_model), hs.reshape(bsz, 1, d_model))
            + stacked(0) + stacked(1) + stacked(2))
```

```python
import functools
import math

import jax
import jax.numpy as jnp
from jax import lax
from jax.experimental import pallas as pl
from jax.experimental.pallas import tpu as pltpu

F32 = jnp.float32
BF16 = jnp.bfloat16

RMS_EPS = 1e-6
ROPE_THETA = 10000.0
NEG_BIG = -1e30

LANES = 128
VMEM_LIMIT_BYTES = 56 * 1024 * 1024

SWA_HEAD_DIM = 64
SWA_GROUP = 8
SWA_WINDOW = 128
GDN_HEAD_DIM = 128
GDN_V_PER_QK = 2
GDN_CONV = 4
GDN_CHUNK = 64
MLA_HEADS = 16
MLA_Q_LORA = 512
MLA_KV_LORA = 512
MLA_NOPE = 128
MLA_ROPE = 64
MLA_V = 128


def _params(*sem):
    return pltpu.CompilerParams(dimension_semantics=sem, vmem_limit_bytes=VMEM_LIMIT_BYTES)


def _rms(y, g):
    return y * lax.rsqrt(jnp.mean(y * y, axis=-1, keepdims=True) + RMS_EPS) * g


def _dot(a, b):
    return jnp.dot(a, b, preferred_element_type=F32)


def _dot_nt(a, b):
    return lax.dot_general(a, b, (((1,), (1,)), ((), ())), preferred_element_type=F32)


def _dot_tn(a, b):
    return lax.dot_general(a, b, (((0,), (0,)), ((), ())), preferred_element_type=F32)


def _bmm(a, b):
    return jnp.einsum("nmk,nkp->nmp", a, b, preferred_element_type=F32)


def _bmm_nt(a, b):
    return jnp.einsum("nmk,npk->nmp", a, b, preferred_element_type=F32)


def _sigmoid(x):
    return 1.0 / (1.0 + jnp.exp(-x))


def _silu(x):
    return x * _sigmoid(x)


def _softplus(x):
    return jnp.maximum(x, 0.0) + jnp.log(1.0 + jnp.exp(-jnp.abs(x)))


def _rope_block(y, cos, sin_signed):
    lane = lax.broadcasted_iota(jnp.int32, (1, LANES), 1)
    first = (lane % 64) < 32
    swapped = jnp.where(first, pltpu.roll(y, 96, 1), pltpu.roll(y, 32, 1))
    return y * cos + swapped * sin_signed


def _rope_tables(pos):
    half = SWA_HEAD_DIM // 2
    inv_freq = ROPE_THETA ** (-jnp.arange(half, dtype=F32) * (2.0 / SWA_HEAD_DIM))
    ang = pos.astype(F32)[:, None] * inv_freq[None, :]
    cos, sin = jnp.cos(ang), jnp.sin(ang)
    return jnp.tile(cos, (1, 4)), jnp.tile(jnp.concatenate([-sin, sin], axis=1), (1, 2))


def _rmsnorm_kernel(x_ref, g_ref, o_ref):
    o_ref[...] = _rms(x_ref[...], g_ref[...]).astype(o_ref.dtype)


def _rmsnorm(x, g, tm):
    m, d = x.shape
    return pl.pallas_call(
        _rmsnorm_kernel,
        grid=(m // tm,),
        in_specs=[pl.BlockSpec((tm, d), lambda i: (i, 0)), pl.BlockSpec((1, d), lambda i: (0, 0))],
        out_specs=pl.BlockSpec((tm, d), lambda i: (i, 0)),
        out_shape=jax.ShapeDtypeStruct((m, d), BF16),
        compiler_params=_params("parallel"),
        name="rmsnorm",
    )(x, g.reshape(1, d))


def _mm_kernel(*refs, has_bias, rope_cols, tn):
    a_ref, w_ref = refs[0], refs[1]
    idx = 2
    if has_bias:
        b_ref = refs[idx]
        idx += 1
    if rope_cols is not None:
        cos_ref, sin_ref = refs[idx], refs[idx + 1]
        idx += 2
    o_ref = refs[idx]
    y = _dot(a_ref[...], w_ref[...])
    if has_bias:
        y = y + b_ref[...]
    if rope_cols is None:
        o_ref[...] = y.astype(o_ref.dtype)
        return
    j = pl.program_id(1)
    cos, sin = cos_ref[...], sin_ref[...]
    for c in range(tn // LANES):
        col0 = j * tn + c * LANES
        on = (col0 >= rope_cols[0]) & (col0 < rope_cols[1])
        yc = y[:, c * LANES:(c + 1) * LANES]
        o_ref[:, c * LANES:(c + 1) * LANES] = jnp.where(on, _rope_block(yc, cos, sin), yc).astype(o_ref.dtype)


def _matmul(a, w, tm, tn, bias=None, rope=None, rope_cols=None, out_dtype=BF16, name="matmul"):
    m, k = a.shape
    n = w.shape[1]
    assert m % tm == 0 and n % tn == 0, (m, tm, n, tn)
    in_specs = [pl.BlockSpec((tm, k), lambda i, j: (i, 0)), pl.BlockSpec((k, tn), lambda i, j: (0, j))]
    args = [a, w]
    if bias is not None:
        in_specs.append(pl.BlockSpec((1, tn), lambda i, j: (0, j)))
        args.append(bias.reshape(1, n).astype(F32))
    if rope is not None:
        in_specs += [pl.BlockSpec((tm, LANES), lambda i, j: (i, 0))] * 2
        args += list(rope)
    return pl.pallas_call(
        functools.partial(_mm_kernel, has_bias=bias is not None, rope_cols=rope_cols, tn=tn),
        grid=(m // tm, n // tn),
        in_specs=in_specs,
        out_specs=pl.BlockSpec((tm, tn), lambda i, j: (i, j)),
        out_shape=jax.ShapeDtypeStruct((m, n), out_dtype),
        compiler_params=_params("parallel", "arbitrary"),
        name=name,
    )(*args)


def _proj_res_kernel(*refs, has_up, has_next, nk):
    if has_up:
        x_ref, wu_ref, w_ref, h_ref, gp_ref = refs[:5]
        idx = 5
    else:
        x_ref, w_ref, h_ref, gp_ref = refs[:4]
        idx = 4
    if has_next:
        gn_ref = refs[idx]
        idx += 1
    ho_ref = refs[idx]
    idx += 1
    if has_next:
        ao_ref = refs[idx]
        idx += 1
    acc_ref = refs[idx] if nk > 1 else None

    if has_up:
        u = jnp.maximum(_dot(x_ref[...], wu_ref[...]), 0.0)
        part = _dot((u * u).astype(BF16), w_ref[...])
    else:
        part = _dot(x_ref[...], w_ref[...])

    def finish(y):
        hn = h_ref[...] + _rms(y, gp_ref[...])
        ho_ref[...] = hn
        if has_next:
            ao_ref[...] = _rms(hn, gn_ref[...]).astype(BF16)

    if nk == 1:
        finish(part)
        return
    k = pl.program_id(1)

    @pl.when(k == 0)
    def _():
        acc_ref[...] = part

    @pl.when(k > 0)
    def _():
        acc_ref[...] += part

    @pl.when(k == nk - 1)
    def _():
        finish(acc_ref[...])


def _proj_res(x, w, h, g_post, g_next, tm, tk, w_up=None, layer=0, name="proj_res"):
    m, d = h.shape
    kdim = w.shape[1]
    nk = kdim // tk
    has_up, has_next = w_up is not None, g_next is not None
    if has_up:
        dk = x.shape[1]
        in_specs = [pl.BlockSpec((tm, dk), lambda i, k: (i, 0)),
                    pl.BlockSpec((None, dk, tk), lambda i, k: (layer, 0, k)),
                    pl.BlockSpec((None, tk, d), lambda i, k: (layer, k, 0))]
        args = [x, w_up, w]
    else:
        in_specs = [pl.BlockSpec((tm, tk), lambda i, k: (i, k)),
                    pl.BlockSpec((None, tk, d), lambda i, k: (layer, k, 0))]
        args = [x, w]
    row = pl.BlockSpec((tm, d), lambda i, k: (i, 0))
    vec = pl.BlockSpec((1, d), lambda i, k: (0, 0))
    in_specs += [row, vec]
    args += [h, g_post.reshape(1, d)]
    out_specs, out_shape = [row], [jax.ShapeDtypeStruct((m, d), F32)]
    if has_next:
        in_specs.append(vec)
        args.append(g_next.reshape(1, d))
        out_specs.append(row)
        out_shape.append(jax.ShapeDtypeStruct((m, d), BF16))
    outs = pl.pallas_call(
        functools.partial(_proj_res_kernel, has_up=has_up, has_next=has_next, nk=nk),
        grid=(m // tm, nk),
        in_specs=in_specs,
        out_specs=out_specs,
        out_shape=out_shape,
        scratch_shapes=[pltpu.VMEM((tm, d), F32)] if nk > 1 else [],
        compiler_params=_params("parallel", "arbitrary"),
        name=name,
    )(*args)
    return (outs[0], outs[1]) if has_next else (outs[0], None)


def _softmax_sink_unnorm(s, sink):
    m = jnp.maximum(jnp.max(s, axis=-1, keepdims=True), sink)
    p = jnp.exp(s - m)
    return p, 1.0 / (jnp.sum(p, axis=-1, keepdims=True) + jnp.exp(sink - m))


def _swa_prompt_kernel(sink_ref, q_ref, kp_ref, kc_ref, vp_ref, vc_ref, o_ref, *, nb):
    w = SWA_WINDOW
    n = pl.program_id(0) % nb
    kvh = pl.program_id(1)
    odd = (kvh % 2) == 1
    lane = lax.broadcasted_iota(jnp.int32, (1, LANES), 1)
    mine = (lane >= 64) == odd

    def pair_operand(p_ref, c_ref):
        blk = jnp.concatenate([p_ref[...], c_ref[...]], axis=0).astype(F32)
        own = jnp.where(mine, blk, 0.0)
        other = pltpu.roll(own, 64, 1)
        lo = jnp.where(odd, other, own)
        hi = jnp.where(odd, own, other)
        return jnp.concatenate([lo, hi], axis=0).astype(BF16)

    k2 = pair_operand(kp_ref, kc_ref)
    v2 = pair_operand(vp_ref, vc_ref)
    qi = lax.broadcasted_iota(jnp.int32, (w, 2 * w), 0)
    kj = lax.broadcasted_iota(jnp.int32, (w, 2 * w), 1)
    rel = qi + w - kj
    mask = (rel >= 0) & (rel < w) & ((n > 0) | (kj >= w))
    for p in range(SWA_GROUP // 2):
        s2 = _dot_nt(q_ref[:, p * LANES:(p + 1) * LANES], k2)
        ps, rs = [], []
        for e in range(2):
            s = jnp.where(mask, s2[:, e * 2 * w:(e + 1) * 2 * w], NEG_BIG)
            pe, re = _softmax_sink_unnorm(s, sink_ref[kvh * SWA_GROUP + 2 * p + e])
            ps.append(pe.astype(BF16))
            rs.append(re)
        o2 = _dot(jnp.concatenate(ps, axis=1), v2)
        o_ref[:, p * LANES:(p + 1) * LANES] = (o2 * jnp.where(lane < 64, rs[0], rs[1])).astype(o_ref.dtype)


def _swa_prompt_attn(qkv, sinks, batch, seq):
    w = SWA_WINDOW
    nb = seq // w
    n_heads = sinks.shape[0]
    kvh = n_heads // SWA_GROUP
    q_dim = n_heads * SWA_HEAD_DIM
    kblk0 = q_dim // LANES
    vblk0 = (q_dim + kvh * SWA_HEAD_DIM) // LANES

    def prev(i):
        return jnp.where(i % nb == 0, i, i - 1)

    return pl.pallas_call(
        functools.partial(_swa_prompt_kernel, nb=nb),
        grid=(batch * nb, kvh),
        in_specs=[
            pl.BlockSpec(memory_space=pltpu.SMEM),
            pl.BlockSpec((w, SWA_GROUP * SWA_HEAD_DIM), lambda i, h: (i, h)),
            pl.BlockSpec((w, LANES), lambda i, h: (prev(i), kblk0 + h // 2)),
            pl.BlockSpec((w, LANES), lambda i, h: (i, kblk0 + h // 2)),
            pl.BlockSpec((w, LANES), lambda i, h: (prev(i), vblk0 + h // 2)),
            pl.BlockSpec((w, LANES), lambda i, h: (i, vblk0 + h // 2)),
        ],
        out_specs=pl.BlockSpec((w, SWA_GROUP * SWA_HEAD_DIM), lambda i, h: (i, h)),
        out_shape=jax.ShapeDtypeStruct((batch * seq, q_dim), BF16),
        compiler_params=_params("parallel", "arbitrary"),
        name="swa_prompt_attn",
    )(sinks.astype(F32), qkv, qkv, qkv, qkv, qkv)


def _swa_sample_kernel(q_ref, kn_ref, vn_ref, ck_ref, cv_ref, sink_ref, o_ref, cko_ref, cvo_ref, *, bt, npair):
    p_rows = ck_ref.shape[1]
    lane = lax.broadcasted_iota(jnp.int32, (1, LANES), 1)
    row = lax.broadcasted_iota(jnp.int32, (p_rows, 1), 0)
    grp = lax.broadcasted_iota(jnp.int32, (2 * SWA_GROUP, 1), 0)
    for b in range(bt):
        k_new, v_new = kn_ref[b], vn_ref[b]
        cko_ref[b, pl.ds(0, p_rows - 1), :] = ck_ref[b, pl.ds(1, p_rows - 1), :]
        cvo_ref[b, pl.ds(0, p_rows - 1), :] = cv_ref[b, pl.ds(1, p_rows - 1), :]
        cko_ref[b, pl.ds(p_rows - 1, 1), :] = k_new
        cvo_ref[b, pl.ds(p_rows - 1, 1), :] = v_new
        for c in range(npair):
            sl = slice(c * LANES, (c + 1) * LANES)
            kk = jnp.where(row == 0, k_new[:, sl], ck_ref[b, :, sl]).astype(BF16)
            vv = jnp.where(row == 0, v_new[:, sl], cv_ref[b, :, sl]).astype(BF16)
            q2 = q_ref[b, pl.ds(c * 2 * SWA_GROUP, 2 * SWA_GROUP), :]
            q2 = jnp.where((lane >= 64) == (grp >= SWA_GROUP), q2, 0.0).astype(BF16)
            s = _dot_nt(q2, kk)
            pe, re = _softmax_sink_unnorm(s, sink_ref[c])
            o2 = _dot(pe.astype(BF16), vv) * re
            o_ref[b, pl.ds(c * SWA_GROUP, SWA_GROUP), :] = jnp.where(
                lane < 64, o2[:SWA_GROUP], o2[SWA_GROUP:]).astype(o_ref.dtype)


def _swa_sample_attn(qkv, ck, cv, sinks):
    bsz, p_rows, kvh, hd = ck.shape
    assert p_rows == SWA_WINDOW and hd == SWA_HEAD_DIM
    n_heads = kvh * SWA_GROUP
    q_dim, kv_dim = n_heads * hd, kvh * hd
    npair = kvh // 2
    bt = 8
    q = qkv[:, :q_dim].reshape(bsz, npair, 2, SWA_GROUP, hd).transpose(0, 1, 3, 2, 4)
    q = jnp.broadcast_to(q.reshape(bsz, npair, 1, SWA_GROUP, 2 * hd), (bsz, npair, 2, SWA_GROUP, 2 * hd))
    q = q.reshape(bsz, npair * 2 * SWA_GROUP, 2 * hd)
    k_new = qkv[:, q_dim:q_dim + kv_dim].reshape(bsz, 1, kv_dim)
    v_new = qkv[:, q_dim + kv_dim:].reshape(bsz, 1, kv_dim)
    sink = sinks.astype(F32).reshape(npair, 2 * SWA_GROUP, 1)
    o, cko, cvo = pl.pallas_call(
        functools.partial(_swa_sample_kernel, bt=bt, npair=npair),
        grid=(bsz // bt,),
        in_specs=[
            pl.BlockSpec((bt, npair * 2 * SWA_GROUP, LANES), lambda i: (i, 0, 0)),
            pl.BlockSpec((bt, 1, kv_dim), lambda i: (i, 0, 0)),
            pl.BlockSpec((bt, 1, kv_dim), lambda i: (i, 0, 0)),
            pl.BlockSpec((bt, p_rows, kv_dim), lambda i: (i, 0, 0)),
            pl.BlockSpec((bt, p_rows, kv_dim), lambda i: (i, 0, 0)),
            pl.BlockSpec((npair, 2 * SWA_GROUP, 1), lambda i: (0, 0, 0)),
        ],
        out_specs=[
            pl.BlockSpec((bt, npair * SWA_GROUP, LANES), lambda i: (i, 0, 0)),
            pl.BlockSpec((bt, p_rows, kv_dim), lambda i: (i, 0, 0)),
            pl.BlockSpec((bt, p_rows, kv_dim), lambda i: (i, 0, 0)),
        ],
        out_shape=[
            jax.ShapeDtypeStruct((bsz, npair * SWA_GROUP, LANES), BF16),
            jax.ShapeDtypeStruct((bsz, p_rows, kv_dim), F32),
            jax.ShapeDtypeStruct((bsz, p_rows, kv_dim), F32),
        ],
        compiler_params=_params("parallel"),
        name="swa_sample_attn",
    )(q, k_new, v_new, ck.reshape(bsz, p_rows, kv_dim), cv.reshape(bsz, p_rows, kv_dim), sink)
    o = o.reshape(bsz, npair, SWA_GROUP, 2, hd).transpose(0, 1, 3, 2, 4).reshape(bsz, q_dim)
    return o, cko.reshape(bsz, p_rows, kvh, hd), cvo.reshape(bsz, p_rows, kvh, hd)


def _gdn_gates_kernel(ba_ref, alog_ref, dt_ref, beta_ref, g_ref, gc_ref, *, chunk, nh):
    rows = ba_ref.shape[0]
    x = ba_ref[...]
    beta_ref[...] = _sigmoid(x)
    a = pltpu.roll(x, LANES - nh, 1)
    g = -jnp.exp(alog_ref[...]) * _softplus(a + dt_ref[...])
    g_ref[...] = g
    pos = lax.broadcasted_iota(jnp.int32, (rows, 1), 0) % chunk
    acc = g
    shift = 1
    while shift < chunk:
        acc = acc + jnp.where(pos >= shift, pltpu.roll(acc, shift, 0), 0.0)
        shift *= 2
    gc_ref[...] = acc


def _gdn_gates(ba, a_log, dt_bias, tr, chunk):
    m = ba.shape[0]
    nh = a_log.shape[0]
    row = pl.BlockSpec((tr, LANES), lambda i: (i, 0))
    vec = pl.BlockSpec((1, LANES), lambda i: (0, 0))

    def pad(x):
        return jnp.pad(x.astype(F32), (0, LANES - nh)).reshape(1, LANES)

    return pl.pallas_call(
        functools.partial(_gdn_gates_kernel, chunk=chunk, nh=nh),
        grid=(m // tr,),
        in_specs=[row, vec, vec],
        out_specs=[row, row, row],
        out_shape=[jax.ShapeDtypeStruct((m, LANES), F32)] * 3,
        compiler_params=_params("parallel"),
        name="gdn_gates",
    )(ba, pad(a_log), pad(dt_bias))


def _gdn_conv_kernel(x_ref, halo_ref, w_ref, o_ref, ext_ref, *, seq, key_dim, scale):
    tr, tc = x_ref.shape
    i, j = pl.program_id(0), pl.program_id(1)
    hb = halo_ref.shape[0]
    starts_seq = (i * tr) % seq == 0
    ext_ref[pl.ds(0, 8), :] = jnp.where(starts_seq, 0.0, halo_ref[pl.ds(hb - 8, 8), :].astype(F32))
    ext_ref[pl.ds(8, tr), :] = x_ref[...].astype(F32)
    acc = ext_ref[pl.ds(8 - (GDN_CONV - 1), tr), :] * w_ref[0:1, :]
    for t in range(1, GDN_CONV):
        acc = acc + ext_ref[pl.ds(8 - (GDN_CONV - 1) + t, tr), :] * w_ref[t:t + 1, :]
    c = _silu(acc)
    col0 = j * tc
    is_q = col0 < key_dim
    is_qk = col0 < 2 * key_dim
    for hblk in range(tc // LANES):
        ch = c[:, hblk * LANES:(hblk + 1) * LANES]
        nrm = ch * lax.rsqrt(jnp.sum(ch * ch, axis=-1, keepdims=True) + RMS_EPS)
        nrm = jnp.where(is_q, nrm * scale, nrm)
        o_ref[:, hblk * LANES:(hblk + 1) * LANES] = jnp.where(is_qk, nrm, ch).astype(o_ref.dtype)


def _gdn_conv(proj, conv_w, seq, key_dim, conv_dim, tr, tc):
    m = proj.shape[0]
    hb = 16
    return pl.pallas_call(
        functools.partial(_gdn_conv_kernel, seq=seq, key_dim=key_dim, scale=GDN_HEAD_DIM ** -0.5),
        grid=(m // tr, conv_dim // tc),
        in_specs=[
            pl.BlockSpec((tr, tc), lambda i, j: (i, j)),
            pl.BlockSpec((hb, tc), lambda i, j: (jnp.maximum(i * (tr // hb) - 1, 0), j)),
            pl.BlockSpec((GDN_CONV, tc), lambda i, j: (0, j)),
        ],
        out_specs=pl.BlockSpec((tr, tc), lambda i, j: (i, j)),
        out_shape=jax.ShapeDtypeStruct((m, conv_dim), BF16),
        scratch_shapes=[pltpu.VMEM((tr + 8, tc), F32)],
        compiler_params=_params("parallel", "arbitrary"),
        name="gdn_conv",
    )(proj, proj, conv_w.astype(F32))


GDN_INV_BASE = 4
GDN_QK_PER_STEP = 8


def _unit_lower_inverse(a):
    n, c, _ = a.shape
    base = GDN_INV_BASE
    ri = lax.broadcasted_iota(jnp.int32, (1, c, c), 1)
    ci = lax.broadcasted_iota(jnp.int32, (1, c, c), 2)
    x = jnp.broadcast_to((ri == ci).astype(F32), (n, c, c))
    diag_blocks = jnp.where(ri // base == ci // base, a, 0.0)
    col_in_block = ci % base
    for j in range(base - 1):
        coef = jnp.sum(jnp.where(col_in_block == j, diag_blocks, 0.0), axis=2, keepdims=True)
        rows = jnp.concatenate(
            [jnp.broadcast_to(x[:, b * base + j:b * base + j + 1, :], (n, base, c)) for b in range(c // base)], axis=1)
        x = x - coef * rows
    size = base
    while size < c:
        off = jnp.where((ri // (2 * size) == ci // (2 * size)) & (ri // size != ci // size), a, 0.0)
        xb = x.astype(BF16)
        x = x - _bmm(_bmm(xb, off.astype(BF16)).astype(BF16), xb)
        size *= 2
    return x


def _gdn_scan_kernel(q_ref, k_ref, v_ref, z_ref, gc_ref, beta_ref, gt_ref, nw_ref, y_ref, so_ref, s_ref, *, chunk):
    rows = q_ref.shape[0]
    hd = GDN_HEAD_DIM
    nc = rows // chunk
    n_qk, per = GDN_QK_PER_STEP, GDN_V_PER_QK
    n_v = n_qk * per
    hq0, t = pl.program_id(1) * n_qk, pl.program_id(2)

    @pl.when(t == 0)
    def _():
        s_ref[...] = jnp.zeros_like(s_ref)

    lane = lax.broadcasted_iota(jnp.int32, (1, LANES), 1)
    ri = lax.broadcasted_iota(jnp.int32, (1, chunk, chunk), 1)
    ci = lax.broadcasted_iota(jnp.int32, (1, chunk, chunk), 2)
    tri, strict = ri >= ci, ri > ci
    q3 = [q_ref[:, h * hd:(h + 1) * hd].reshape(nc, chunk, hd) for h in range(n_qk)]
    k3 = [k_ref[:, h * hd:(h + 1) * hd].reshape(nc, chunk, hd) for h in range(n_qk)]
    gkk = [_bmm_nt(k3[h], k3[h]) for h in range(n_qk)]
    gqk = [_bmm_nt(q3[h], k3[h]) for h in range(n_qk)]
    bcol, eg, w_last, eg_last, lower, qk = [], [], [], [], [], []
    for i in range(n_v):
        pick = lane == hq0 * per + i
        gcol = jnp.sum(jnp.where(pick, gc_ref[...], 0.0), axis=1, keepdims=True).reshape(nc, chunk, 1)
        bcol.append(jnp.sum(jnp.where(pick, beta_ref[...], 0.0), axis=1, keepdims=True).reshape(nc, chunk, 1))
        grow_full = gt_ref[pl.ds(hq0 * per + i, 1), :]
        grow = jnp.stack([grow_full[:, c * chunk:(c + 1) * chunk] for c in range(nc)], axis=0)
        decay = jnp.where(tri, jnp.exp(jnp.where(tri, gcol - grow, 0.0)), 0.0)
        lower.append(jnp.where(strict, bcol[i] * gkk[i // per] * decay, 0.0))
        qk.append((gqk[i // per] * decay).astype(BF16))
        g_last = gcol[:, chunk - 1:chunk, :]
        eg.append(jnp.exp(gcol))
        w_last.append(jnp.exp(g_last - gcol))
        eg_last.append(jnp.exp(g_last))
    tinv = _unit_lower_inverse(jnp.concatenate(lower, axis=0)).astype(BF16)

    state = [s_ref[h] for h in range(n_qk)]
    for c in range(nc):
        r0 = c * chunk
        sb = [state[h].astype(BF16) for h in range(n_qk)]
        ks = [_dot(k3[h][c], sb[h]) for h in range(n_qk)]
        qs = [_dot(q3[h][c], sb[h]) for h in range(n_qk)]
        rhs = []
        for i in range(n_v):
            h, e = divmod(i, per)
            v = v_ref[pl.ds(r0, chunk), pl.ds(i * hd, hd)].astype(F32)
            rhs.append((bcol[i][c] * (v - eg[i][c] * ks[h][:, e * hd:(e + 1) * hd])).astype(BF16))
        v_new = _bmm(jnp.stack([tinv[i * nc + c] for i in range(n_v)]), jnp.stack(rhs))
        o_intra = _bmm(jnp.stack([qk[i][c] for i in range(n_v)]), v_new.astype(BF16))
        for i in range(n_v):
            h, e = divmod(i, per)
            o = eg[i][c] * qs[h][:, e * hd:(e + 1) * hd] + o_intra[i]
            gate = _silu(z_ref[pl.ds(r0, chunk), pl.ds(i * hd, hd)].astype(F32))
            y_ref[pl.ds(r0, chunk), pl.ds(i * hd, hd)] = (_rms(o, nw_ref[...]) * gate).astype(y_ref.dtype)
        for h in range(n_qk):
            mine = range(h * per, (h + 1) * per)
            wv = jnp.concatenate([(w_last[i][c] * v_new[i]).astype(BF16) for i in mine], axis=1)
            keep = jnp.concatenate([jnp.broadcast_to(eg_last[i][c], (1, hd)) for i in mine], axis=1)
            state[h] = keep * state[h] + _dot_tn(k3[h][c], wv)
    for h in range(n_qk):
        s_ref[h] = state[h]

    @pl.when(t == pl.num_programs(2) - 1)
    def _():
        for i in range(n_v):
            h, e = divmod(i, per)
            so_ref[0, i] = state[h][:, e * hd:(e + 1) * hd]


def _gdn_scan(qkv, proj, gc, beta, gc_t, norm_w, batch, seq, n_qk, rows, chunk):
    hd = GDN_HEAD_DIM
    m = batch * seq
    nt = seq // rows
    n_v = n_qk * GDN_V_PER_QK
    qw = GDN_QK_PER_STEP * hd
    n_local = GDN_QK_PER_STEP * GDN_V_PER_QK
    vw = n_local * hd
    k_blk0 = n_qk * hd // qw
    v_blk0 = 2 * n_qk * hd // vw
    z_blk0 = (2 * n_qk * hd + n_v * hd) // vw
    y, s_fin = pl.pallas_call(
        functools.partial(_gdn_scan_kernel, chunk=chunk),
        grid=(batch, n_qk // GDN_QK_PER_STEP, nt),
        in_specs=[
            pl.BlockSpec((rows, qw), lambda b, h, t: (b * nt + t, h)),
            pl.BlockSpec((rows, qw), lambda b, h, t: (b * nt + t, k_blk0 + h)),
            pl.BlockSpec((rows, vw), lambda b, h, t: (b * nt + t, v_blk0 + h)),
            pl.BlockSpec((rows, vw), lambda b, h, t: (b * nt + t, z_blk0 + h)),
            pl.BlockSpec((rows, LANES), lambda b, h, t: (b * nt + t, 0)),
            pl.BlockSpec((rows, LANES), lambda b, h, t: (b * nt + t, 0)),
            pl.BlockSpec((n_v, rows), lambda b, h, t: (0, b * nt + t)),
            pl.BlockSpec((1, hd), lambda b, h, t: (0, 0)),
        ],
        out_specs=[
            pl.BlockSpec((rows, vw), lambda b, h, t: (b * nt + t, h)),
            pl.BlockSpec((1, n_local, hd, hd), lambda b, h, t: (b, h, 0, 0)),
        ],
        out_shape=[
            jax.ShapeDtypeStruct((m, n_v * hd), BF16),
            jax.ShapeDtypeStruct((batch, n_v, hd, hd), F32),
        ],
        scratch_shapes=[pltpu.VMEM((GDN_QK_PER_STEP, hd, GDN_V_PER_QK * hd), F32)],
        compiler_params=_params("parallel", "parallel", "arbitrary"),
        name="gdn_scan",
    )(qkv, qkv, qkv, proj, gc, beta, gc_t, norm_w.reshape(1, hd).astype(F32))
    return y, s_fin


def _gdn_conv_step_kernel(s0_ref, s1_ref, s2_ref, x_ref, w_ref, o_ref, *, key_dim, scale):
    tc = x_ref.shape[1]
    acc = s0_ref[...] * w_ref[0:1, :] + s1_ref[...] * w_ref[1:2, :] + s2_ref[...] * w_ref[2:3, :] \
        + x_ref[...] * w_ref[3:4, :]
    c = _silu(acc)
    col0 = pl.program_id(0) * tc
    is_q = col0 < key_dim
    is_qk = col0 < 2 * key_dim
    for hblk in range(tc // LANES):
        ch = c[:, hblk * LANES:(hblk + 1) * LANES]
        nrm = ch * lax.rsqrt(jnp.sum(ch * ch, axis=-1, keepdims=True) + RMS_EPS)
        nrm = jnp.where(is_q, nrm * scale, nrm)
        o_ref[:, hblk * LANES:(hblk + 1) * LANES] = jnp.where(is_qk, nrm, ch)


def _gdn_conv_step(conv_state, mixed, conv_w, key_dim, tc):
    bsz, taps, cdim = conv_state.shape
    assert taps == GDN_CONV - 1
    ncol = cdim // tc
    flat = conv_state.reshape(bsz, taps * cdim)
    return pl.pallas_call(
        functools.partial(_gdn_conv_step_kernel, key_dim=key_dim, scale=GDN_HEAD_DIM ** -0.5),
        grid=(ncol,),
        in_specs=[
            pl.BlockSpec((bsz, tc), lambda j: (0, j)),
            pl.BlockSpec((bsz, tc), lambda j: (0, ncol + j)),
            pl.BlockSpec((bsz, tc), lambda j: (0, 2 * ncol + j)),
            pl.BlockSpec((bsz, tc), lambda j: (0, j)),
            pl.BlockSpec((GDN_CONV, tc), lambda j: (0, j)),
        ],
        out_specs=pl.BlockSpec((bsz, tc), lambda j: (0, j)),
        out_shape=jax.ShapeDtypeStruct((bsz, cdim), F32),
        compiler_params=_params("parallel"),
        name="gdn_conv_step",
    )(flat, flat, flat, mixed, conv_w.astype(F32))


def _gdn_step_kernel(q_ref, k_ref, v_ref, z_ref, g_ref, beta_ref, nw_ref, s_ref, y_ref, so_ref, o_scr):
    bsz = q_ref.shape[0]
    hv = pl.program_id(0)
    pick = lax.broadcasted_iota(jnp.int32, (1, LANES), 1) == hv
    egcol = jnp.exp(jnp.sum(jnp.where(pick, g_ref[...], 0.0), axis=1, keepdims=True))
    bcol = jnp.sum(jnp.where(pick, beta_ref[...], 0.0), axis=1, keepdims=True)
    q_t = q_ref[...].T
    k_t = k_ref[...].T
    for b in range(bsz):
        kc, qc = k_t[:, b:b + 1], q_t[:, b:b + 1]
        s1 = s_ref[b, 0] * egcol[b:b + 1, :]
        delta = (v_ref[b:b + 1, :] - jnp.sum(s1 * kc, axis=0, keepdims=True)) * bcol[b:b + 1, :]
        s2 = s1 + kc * delta
        so_ref[b, 0] = s2
        o_scr[b:b + 1, :] = jnp.sum(s2 * qc, axis=0, keepdims=True)
    y_ref[...] = (_rms(o_scr[...], nw_ref[...]) * _silu(z_ref[...])).astype(y_ref.dtype)


def _gdn_step(qkv, proj, g, beta, norm_w, ssm, n_qk):
    bsz, n_v, hd, _ = ssm.shape
    k_blk0, v_blk0 = n_qk, 2 * n_qk
    z_blk0 = 2 * n_qk + n_v
    row = pl.BlockSpec((bsz, LANES), lambda h: (0, 0))
    st = pl.BlockSpec((bsz, 1, hd, hd), lambda h: (0, h, 0, 0))
    return pl.pallas_call(
        _gdn_step_kernel,
        grid=(n_v,),
        in_specs=[
            pl.BlockSpec((bsz, hd), lambda h: (0, h // GDN_V_PER_QK)),
            pl.BlockSpec((bsz, hd), lambda h: (0, k_blk0 + h // GDN_V_PER_QK)),
            pl.BlockSpec((bsz, hd), lambda h: (0, v_blk0 + h)),
            pl.BlockSpec((bsz, hd), lambda h: (0, z_blk0 + h)),
            row, row,
            pl.BlockSpec((1, hd), lambda h: (0, 0)),
            st,
        ],
        out_specs=[pl.BlockSpec((bsz, hd), lambda h: (0, h)), st],
        out_shape=[jax.ShapeDtypeStruct((bsz, n_v * hd), BF16), jax.ShapeDtypeStruct(ssm.shape, F32)],
        scratch_shapes=[pltpu.VMEM((bsz, hd), F32)],
        compiler_params=_params("parallel"),
        name="gdn_step",
    )(qkv, qkv, qkv, proj, g, beta, norm_w.reshape(1, hd).astype(F32), ssm)


MLA_IN_COLS = MLA_Q_LORA + MLA_KV_LORA + 2 * LANES
MLA_Q_COLS = MLA_HEADS * (MLA_NOPE + MLA_ROPE)
MLA_QK_W = MLA_NOPE + LANES
MLA_KV_W = MLA_QK_W + MLA_V
LOG2E = math.log2(math.e)


def _mla_proj_kernel(*refs, has_kv):
    a_ref, win_ref, qn_ref, kvn_ref, wuq_ref = refs[:5]
    idx = 5
    if has_kv:
        wukv_ref = refs[idx]
        idx += 1
    cos_ref, sin_ref, q_ref = refs[idx:idx + 3]
    idx += 3
    if has_kv:
        kv_ref = refs[idx]
        idx += 1
    ckv_ref, kr_ref = refs[idx:idx + 2]
    cos, sin = cos_ref[...], sin_ref[...]
    p = _dot(a_ref[...], win_ref[...])
    cq = _rms(p[:, :MLA_Q_LORA], qn_ref[...]).astype(BF16)
    ckv = _rms(p[:, MLA_Q_LORA:MLA_Q_LORA + MLA_KV_LORA], kvn_ref[...])
    ckv_ref[...] = ckv
    r0 = MLA_Q_LORA + MLA_KV_LORA
    kr_ref[...] = _rope_block(p[:, r0:r0 + LANES], cos, sin)
    q = _dot(cq, wuq_ref[...])
    nope = MLA_HEADS * MLA_NOPE
    q_rope = [_rope_block(q[:, nope + c * LANES:nope + (c + 1) * LANES], cos, sin).astype(q_ref.dtype)
              for c in range(MLA_HEADS * MLA_ROPE // LANES)]
    if not has_kv:
        q_ref[:, :nope] = q[:, :nope].astype(q_ref.dtype)
        for c, qr in enumerate(q_rope):
            q_ref[:, nope + c * LANES:nope + (c + 1) * LANES] = qr
        return
    k_rope = [kr_ref[...].astype(kv_ref.dtype),
              _rope_block(p[:, r0 + LANES:r0 + 2 * LANES], cos, sin).astype(kv_ref.dtype)]
    kv = _dot(ckv.astype(BF16), wukv_ref[...])
    hd = MLA_NOPE
    for h in range(MLA_HEADS):
        q_ref[:, MLA_QK_W * h:MLA_QK_W * h + hd] = q[:, hd * h:hd * (h + 1)].astype(q_ref.dtype)
        q_ref[:, MLA_QK_W * h + hd:MLA_QK_W * (h + 1)] = q_rope[h // 2]
        k0 = MLA_KV_W * h
        kv_ref[:, k0:k0 + hd] = kv[:, 2 * hd * h:2 * hd * h + hd].astype(kv_ref.dtype)
        kv_ref[:, k0 + hd:k0 + MLA_QK_W] = k_rope[h % 2]
        kv_ref[:, k0 + MLA_QK_W:k0 + MLA_KV_W] = kv[:, 2 * hd * h + hd:2 * hd * (h + 1)].astype(kv_ref.dtype)


def _mla_project(a, w_in, q_norm, w_uq, kv_norm, rope, tm, w_ukv=None):
    m, d = a.shape
    has_kv = w_ukv is not None
    row = lambda n: pl.BlockSpec((tm, n), lambda i: (i, 0))
    full = lambda s: pl.BlockSpec(s, lambda i: (0, 0))
    in_specs = [row(d), full(w_in.shape), full((1, MLA_Q_LORA)), full((1, MLA_KV_LORA)), full(w_uq.shape)]
    args = [a, w_in, q_norm.reshape(1, -1).astype(F32), kv_norm.reshape(1, -1).astype(F32), w_uq]
    q_cols = MLA_HEADS * MLA_QK_W if has_kv else MLA_Q_COLS
    out_specs, out_shape = [row(q_cols)], [jax.ShapeDtypeStruct((m, q_cols), BF16)]
    if has_kv:
        in_specs.append(full(w_ukv.shape))
        args.append(w_ukv)
        out_specs.append(row(MLA_HEADS * MLA_KV_W))
        out_shape.append(jax.ShapeDtypeStruct((m, MLA_HEADS * MLA_KV_W), BF16))
    in_specs += [row(LANES), row(LANES)]
    args += list(rope)
    out_specs += [row(MLA_KV_LORA), row(LANES)]
    out_shape += [jax.ShapeDtypeStruct((m, MLA_KV_LORA), F32), jax.ShapeDtypeStruct((m, LANES), F32)]
    return pl.pallas_call(
        functools.partial(_mla_proj_kernel, has_kv=has_kv),
        grid=(m // tm,),
        in_specs=in_specs,
        out_specs=out_specs,
        out_shape=out_shape,
        compiler_params=_params("parallel"),
        name="mla_project",
    )(*args)


def _mla_attn_kernel(q_ref, kv_ref, o_ref, *, tq):
    qi = pl.program_id(2)
    causal = (lax.broadcasted_iota(jnp.int32, (tq, tq), 1) <= lax.broadcasted_iota(jnp.int32, (tq, tq), 0))

    def tile(j, carry, on_diagonal):
        r0 = pl.multiple_of(j * tq, tq)
        out = []
        for e in range(2):
            m_prev, l_prev, acc = carry[e]
            k = kv_ref[pl.ds(r0, tq), pl.ds(e * MLA_KV_W, MLA_QK_W)]
            v = kv_ref[pl.ds(r0, tq), pl.ds(e * MLA_KV_W + MLA_QK_W, MLA_V)]
            s = _dot_nt(q_ref[:, e * MLA_QK_W:(e + 1) * MLA_QK_W], k)
            if on_diagonal:
                s = jnp.where(causal, s, NEG_BIG)
            m_new = jnp.maximum(m_prev, jnp.max(s, axis=-1, keepdims=True))
            alpha = jnp.exp2(m_prev - m_new)
            p = jnp.exp2(s - m_new)
            out.append((m_new, alpha * l_prev + jnp.sum(p, axis=-1, keepdims=True),
                        alpha * acc + _dot(p.astype(BF16), v)))
        return tuple(out)

    init = tuple((jnp.full((tq, 1), NEG_BIG, F32), jnp.zeros((tq, 1), F32), jnp.zeros((tq, MLA_V), F32))
                 for _ in range(2))
    carry = lax.fori_loop(0, qi, lambda j, c: tile(j, c, False), init)
    carry = tile(qi, carry, True)
    for e in range(2):
        o_ref[:, e * MLA_V:(e + 1) * MLA_V] = (carry[e][2] / carry[e][1]).astype(o_ref.dtype)


def _mla_prompt_attn(q, kv, batch, seq, tq):
    nq = seq // tq
    return pl.pallas_call(
        functools.partial(_mla_attn_kernel, tq=tq),
        grid=(batch, MLA_HEADS // 2, nq),
        in_specs=[
            pl.BlockSpec((tq, 2 * MLA_QK_W), lambda b, p, qi: (b * nq + qi, p)),
            pl.BlockSpec((seq, 2 * MLA_KV_W), lambda b, p, qi: (b, p)),
        ],
        out_specs=pl.BlockSpec((tq, 2 * MLA_V), lambda b, p, qi: (b * nq + qi, p)),
        out_shape=jax.ShapeDtypeStruct((batch * seq, MLA_HEADS * MLA_V), BF16),
        compiler_params=_params("parallel", "parallel", "arbitrary"),
        name="mla_prompt_attn",
    )(q, kv)


def _head_mm_kernel(x_ref, w_ref, o_ref, *, transpose_w):
    y = _dot_nt(x_ref[...], w_ref[...]) if transpose_w else _dot(x_ref[...], w_ref[...])
    o_ref[...] = y.astype(o_ref.dtype)


def _head_matmul(x, w_ukv, transpose_w):
    bsz = x.shape[0]
    xw = x.shape[1] // MLA_HEADS
    ow = MLA_KV_LORA if transpose_w else MLA_V
    return pl.pallas_call(
        functools.partial(_head_mm_kernel, transpose_w=transpose_w),
        grid=(MLA_HEADS,),
        in_specs=[
            pl.BlockSpec((bsz, xw), lambda h: (0, h)),
            pl.BlockSpec((MLA_KV_LORA, LANES), lambda h: (0, 2 * h if transpose_w else 2 * h + 1)),
        ],
        out_specs=pl.BlockSpec((bsz, ow), lambda h: (0, h)),
        out_shape=jax.ShapeDtypeStruct((bsz, MLA_HEADS * ow), BF16),
        compiler_params=_params("parallel"),
        name="mla_head_matmul",
    )(x, w_ukv)


MLA_DECODE_STREAMS = 2
MLA_DECODE_SLOTS = 3


def _mla_decode_kernel(pt_ref, ql_ref, qr_ref, ckv_ref, krn_ref, lat_hbm, rope_hbm, o_ref,
                       lat_buf, rope_buf, sem, m_ref, l_ref, acc_ref, *, layer, pages):
    b, c = pl.program_id(0), pl.program_id(1)
    n_chunks = pl.num_programs(1)
    n_steps = pl.num_programs(0) * n_chunks
    step = b * n_chunks + c
    n_slots = lat_buf.shape[0]
    ahead = n_slots - 1
    slot = step % n_slots
    page = rope_hbm.shape[3]
    ns = MLA_DECODE_STREAMS
    keys = pages * page // ns

    def page_copies(step_idx, slot_idx):
        bb, cc = step_idx // n_chunks, step_idx % n_chunks
        out = []
        for g in range(pages):
            pg = pt_ref[bb, cc * pages + g]
            out.append(pltpu.make_async_copy(lat_hbm.at[layer, pg], lat_buf.at[slot_idx, pl.ds(g * page, page), :],
                                             sem.at[0, slot_idx]))
            out.append(pltpu.make_async_copy(rope_hbm.at[layer, pg], rope_buf.at[slot_idx, :, pl.ds(g * page, page)],
                                             sem.at[1, slot_idx]))
        return out

    @pl.when(step == 0)
    def _():
        for first in range(ahead):
            for cp in page_copies(first, first):
                cp.start()

    @pl.when(c == 0)
    def _():
        m_ref[...] = jnp.full_like(m_ref, NEG_BIG)
        l_ref[...] = jnp.zeros_like(l_ref)
        acc_ref[...] = jnp.zeros_like(acc_ref)

    @pl.when(step + ahead < n_steps)
    def _():
        for cp in page_copies(step + ahead, (step + ahead) % n_slots):
            cp.start()

    for cp in page_copies(step, slot):
        cp.wait()

    ql, qr = ql_ref[0], qr_ref[0]
    lats, scores = [], []
    for st in range(ns):
        lat = lat_buf[slot, pl.ds(st * keys, keys), :].astype(BF16)
        rope_t = rope_buf[slot, :, pl.ds(st * keys, keys)].astype(BF16)
        lats.append(lat)
        scores.append(_dot_nt(ql, lat) + _dot(qr, rope_t))
    probs = []
    for st in range(ns):
        m_prev = m_ref[st]
        m_new = jnp.maximum(m_prev, jnp.max(scores[st], axis=-1, keepdims=True))
        alpha = jnp.exp2(m_prev - m_new)
        p = jnp.exp2(scores[st] - m_new)
        l_ref[st] = alpha * l_ref[st] + jnp.sum(p, axis=-1, keepdims=True)
        m_ref[st] = m_new
        probs.append((alpha, p.astype(BF16)))
    for st in range(ns):
        acc_ref[st] = probs[st][0] * acc_ref[st] + _dot(probs[st][1], lats[st])

    @pl.when(c == pl.num_programs(1) - 1)
    def _():
        ckv, krn = ckv_ref[0], krn_ref[0]
        s_new = (jnp.sum(ql.astype(F32) * ckv, axis=-1, keepdims=True)
                 + jnp.sum(qr.astype(F32) * krn, axis=-1, keepdims=True))
        m_fin = s_new
        for st in range(ns):
            m_fin = jnp.maximum(m_fin, m_ref[st])
        p_new = jnp.exp2(s_new - m_fin)
        l_fin = p_new
        acc = p_new * ckv
        for st in range(ns):
            w = jnp.exp2(m_ref[st] - m_fin)
            l_fin = l_fin + w * l_ref[st]
            acc = acc + w * acc_ref[st]
        o_ref[0] = (acc / l_fin).astype(o_ref.dtype)


def _mla_decode(q_lat, q_rope, ckv_new, kr_new, lat_cache, rope_cache, page_table, layer, pages):
    bsz, n_pages = page_table.shape
    page = lat_cache.shape[2]
    nh = q_lat.shape[1]
    ns = MLA_DECODE_STREAMS
    assert pages % ns == 0 and n_pages % pages == 0
    assert bsz * (n_pages // pages) >= MLA_DECODE_SLOTS - 1
    rope_cache_t = jnp.swapaxes(rope_cache, 2, 3)

    grid_spec = pltpu.PrefetchScalarGridSpec(
        num_scalar_prefetch=1,
        grid=(bsz, n_pages // pages),
        in_specs=[
            pl.BlockSpec((1, nh, MLA_KV_LORA), lambda b, c, pt: (b, 0, 0)),
            pl.BlockSpec((1, nh, MLA_ROPE), lambda b, c, pt: (b, 0, 0)),
            pl.BlockSpec((1, 1, MLA_KV_LORA), lambda b, c, pt: (b, 0, 0)),
            pl.BlockSpec((1, 1, MLA_ROPE), lambda b, c, pt: (b, 0, 0)),
            pl.BlockSpec(memory_space=pl.ANY),
            pl.BlockSpec(memory_space=pl.ANY),
        ],
        out_specs=pl.BlockSpec((1, nh, MLA_KV_LORA), lambda b, c, pt: (b, 0, 0)),
        scratch_shapes=[
            pltpu.VMEM((MLA_DECODE_SLOTS, pages * page, MLA_KV_LORA), F32),
            pltpu.VMEM((MLA_DECODE_SLOTS, MLA_ROPE, pages * page), F32),
            pltpu.SemaphoreType.DMA((2, MLA_DECODE_SLOTS)),
            pltpu.VMEM((ns, nh, 1), F32), pltpu.VMEM((ns, nh, 1), F32), pltpu.VMEM((ns, nh, MLA_KV_LORA), F32),
        ],
    )
    return pl.pallas_call(
        functools.partial(_mla_decode_kernel, layer=layer, pages=pages),
        grid_spec=grid_spec,
        out_shape=jax.ShapeDtypeStruct((bsz, nh, MLA_KV_LORA), BF16),
        compiler_params=_params("arbitrary", "arbitrary"),
        name="mla_decode",
    )(page_table, q_lat, q_rope, ckv_new, kr_new, lat_cache, rope_cache_t)


PROMPT_TM = 512
FFN_TK = 1024
OUT_TK = 1024
GDN_ROWS = 256
MLA_TQ = 512
MLA_PAGES_PER_STEP = 16


def _swa_weights(w_qkv, b_qkv, n_heads):
    q_dim = n_heads * SWA_HEAD_DIM
    col_scale = jnp.where(jnp.arange(w_qkv.shape[1]) < q_dim, SWA_HEAD_DIM ** -0.5, 1.0).astype(F32)
    return (w_qkv * col_scale).astype(BF16), b_qkv * col_scale


def _swa_layer(a_p, a_s, rope_p, rope_s, w_qkv, b_qkv, sinks, ck, cv, batch, seq):
    n_heads = sinks.shape[0]
    kvh = ck.shape[2]
    q_dim, kv_dim = n_heads * SWA_HEAD_DIM, kvh * SWA_HEAD_DIM
    w, b = _swa_weights(w_qkv, b_qkv, n_heads)
    rope_cols = (0, q_dim + kv_dim)
    qkv_p = _matmul(a_p, w, PROMPT_TM, 512, bias=b, rope=rope_p, rope_cols=rope_cols, name="swa_qkv")
    o_p = _swa_prompt_attn(qkv_p, sinks, batch, seq)
    rows = min(SWA_WINDOW, seq)
    kv_tail = qkv_p.reshape(batch, seq, -1)[:, seq - rows:, q_dim:].astype(F32)
    k_p = kv_tail[..., :kv_dim].reshape(batch, rows, kvh, SWA_HEAD_DIM)
    v_p = kv_tail[..., kv_dim:].reshape(batch, rows, kvh, SWA_HEAD_DIM)
    qkv_s = _matmul(a_s, w, a_s.shape[0], 512, bias=b, rope=rope_s, rope_cols=rope_cols, out_dtype=F32,
                    name="swa_qkv")
    o_s, k_s, v_s = _swa_sample_attn(qkv_s, ck, cv, sinks)
    return o_p, o_s, (k_p, v_p, k_s, v_s)


def _gdn_layer(a_p, a_s, w_in, conv_w, a_log, dt_bias, norm_w, ssm, conv_state, batch, seq):
    n_v = a_log.shape[0]
    n_qk = n_v // GDN_V_PER_QK
    key_dim = n_qk * GDN_HEAD_DIM
    conv_dim = 2 * key_dim + n_v * GDN_HEAD_DIM
    main_cols = conv_dim + n_v * GDN_HEAD_DIM
    w_main = w_in[:, :main_cols].astype(BF16)
    w_ba = jnp.pad(w_in[:, main_cols:], ((0, 0), (0, LANES - 2 * n_v))).astype(BF16)
    proj_p = _matmul(a_p, w_main, PROMPT_TM, 1024, name="gdn_in")
    ba_p = _matmul(a_p, w_ba, PROMPT_TM, LANES, out_dtype=F32, name="gdn_in_gates")
    beta_p, _, gc_p = _gdn_gates(ba_p, a_log, dt_bias, GDN_ROWS, GDN_CHUNK)
    qkv_p = _gdn_conv(proj_p, conv_w, seq, key_dim, conv_dim, GDN_ROWS, 1024)
    y_p, s_p = _gdn_scan(qkv_p, proj_p, gc_p, beta_p, gc_p[:, :n_v].T, norm_w, batch, seq, n_qk, GDN_ROWS, GDN_CHUNK)
    taps = GDN_CONV - 1
    c_p = proj_p.reshape(batch, seq, -1)[:, seq - taps:, :conv_dim].astype(F32)
    bsz = a_s.shape[0]
    proj_s = _matmul(a_s, w_main, bsz, 1024, out_dtype=F32, name="gdn_in")
    ba_s = _matmul(a_s, w_ba, bsz, LANES, out_dtype=F32, name="gdn_in_gates")
    beta_s, g_s, _ = _gdn_gates(ba_s, a_log, dt_bias, bsz, 1)
    qkv_s = _gdn_conv_step(conv_state, proj_s, conv_w, key_dim, 1024)
    y_s, s_s = _gdn_step(qkv_s, proj_s, g_s, beta_s, norm_w, ssm, n_qk)
    c_s = jnp.concatenate([conv_state[:, 1:], proj_s[:, None, :conv_dim]], axis=1)
    return y_p, y_s, (s_p, c_p, s_s, c_s)


def _mla_weights(w_in, w_uq):
    zeros = jnp.zeros((w_in.shape[0], MLA_ROPE), w_in.dtype)
    kr = w_in[:, MLA_Q_LORA + MLA_KV_LORA:]
    w_in_p = jnp.concatenate([w_in[:, :MLA_Q_LORA + MLA_KV_LORA], kr, zeros, zeros, kr], axis=1)
    wq = w_uq.reshape(MLA_Q_LORA, MLA_HEADS, MLA_NOPE + MLA_ROPE) * ((MLA_NOPE + MLA_ROPE) ** -0.5 * LOG2E)
    wq = jnp.concatenate([wq[..., :MLA_NOPE].reshape(MLA_Q_LORA, -1), wq[..., MLA_NOPE:].reshape(MLA_Q_LORA, -1)],
                         axis=1)
    return w_in_p.astype(BF16), wq.astype(BF16)


def _mla_layer(a_p, a_s, rope_p, rope_s, w_in, q_norm, w_uq, kv_norm, w_ukv, lat_cache, rope_cache, page_table,
               layer, batch, seq):
    w_in_p, wq = _mla_weights(w_in, w_uq)
    w_ukv = w_ukv.astype(BF16)
    nope = MLA_HEADS * MLA_NOPE
    q_p, kv_p, ckv_p, kr_p = _mla_project(a_p, w_in_p, q_norm, wq, kv_norm, rope_p, 256, w_ukv=w_ukv)
    o_p = _mla_prompt_attn(q_p, kv_p, batch, seq, min(MLA_TQ, seq))
    bsz = a_s.shape[0]
    q_s, ckv_s, kr_s = _mla_project(a_s, w_in_p, q_norm, wq, kv_norm, rope_s, bsz)
    q_lat = _head_matmul(q_s[:, :nope], w_ukv, True).reshape(bsz, MLA_HEADS, MLA_KV_LORA)
    q_rope = q_s[:, nope:].reshape(bsz, MLA_HEADS, MLA_ROPE)
    kr_new = kr_s[:, :MLA_ROPE]
    o_lat = _mla_decode(q_lat, q_rope, ckv_s.reshape(bsz, 1, -1), kr_new.reshape(bsz, 1, -1), lat_cache, rope_cache,
                        page_table, layer, MLA_PAGES_PER_STEP)
    o_s = _head_matmul(o_lat.reshape(bsz, -1), w_ukv, False)
    outs = (ckv_p.reshape(batch, seq, -1), kr_p[:, :MLA_ROPE].reshape(batch, seq, -1),
            ckv_s.reshape(bsz, 1, -1), kr_new.reshape(bsz, 1, -1))
    return o_p, o_s, outs


def kernel(x_prompt, x_sample, cache_swa_k, cache_swa_v, state_gdn_ssm, state_gdn_conv, cache_mla_latent, cache_mla_krope, page_table, norm_mix_pre, norm_mix_post, norm_ffn_pre, norm_ffn_post, ffn_w_up, ffn_w_down, swa_w_qkv, swa_b_qkv, swa_w_o, swa_sinks, gdn_w_in, gdn_conv_w, gdn_A_log, gdn_dt_bias, gdn_norm, gdn_w_out, mla_w_in, mla_q_norm, mla_w_uq, mla_kv_norm, mla_w_ukv, mla_w_o):
    batch, seq, d_model = x_prompt.shape
    bsz, dec_seq, _ = x_sample.shape
    assert dec_seq == 1
    depth = norm_mix_pre.shape[0]
    past_len = page_table.shape[1] * cache_mla_latent.shape[2]
    hp = x_prompt.reshape(batch * seq, d_model)
    hs = x_sample.reshape(bsz, d_model)
    rope_p = _rope_tables(jnp.tile(jnp.arange(seq, dtype=jnp.int32), batch))
    rope_s = _rope_tables(jnp.full((bsz,), past_len, jnp.int32))
    a_p = _rmsnorm(hp, norm_mix_pre[0], PROMPT_TM)
    a_s = _rmsnorm(hs, norm_mix_pre[0], bsz)
    collected = ([], [], [])
    w_up, w_down = ffn_w_up.astype(BF16), ffn_w_down.astype(BF16)
    w_outs = (swa_w_o.astype(BF16), gdn_w_out.astype(BF16), mla_w_o.astype(BF16))
    for layer in range(depth):
        kind, j = layer % 3, layer // 3
        if kind == 0:
            x_p, x_s, outs = _swa_layer(a_p, a_s, rope_p, rope_s, swa_w_qkv[j], swa_b_qkv[j], swa_sinks[j],
                                        cache_swa_k[j], cache_swa_v[j], batch, seq)
        elif kind == 1:
            x_p, x_s, outs = _gdn_layer(a_p, a_s, gdn_w_in[j], gdn_conv_w[j], gdn_A_log[j], gdn_dt_bias[j],
                                        gdn_norm[j], state_gdn_ssm[j], state_gdn_conv[j], batch, seq)
        else:
            x_p, x_s, outs = _mla_layer(a_p, a_s, rope_p, rope_s, mla_w_in[j], mla_q_norm[j], mla_w_uq[j],
                                        mla_kv_norm[j], mla_w_ukv[j], cache_mla_latent, cache_mla_krope,
                                        page_table, j, batch, seq)
        collected[kind].append(outs)
        w_o = w_outs[kind]
        g_next = norm_mix_pre[layer + 1] if layer + 1 < depth else None
        hp, a_p = _proj_res(x_p, w_o, hp, norm_mix_post[layer], norm_ffn_pre[layer], PROMPT_TM, OUT_TK,
                            layer=j, name="mixer_out")
        hs, a_s = _proj_res(x_s, w_o, hs, norm_mix_post[layer], norm_ffn_pre[layer], bsz, OUT_TK,
                            layer=j, name="mixer_out")
        hp, a_p = _proj_res(a_p, w_down, hp, norm_ffn_post[layer], g_next, PROMPT_TM, FFN_TK, w_up=w_up,
                            layer=layer, name="ffn")
        hs, a_s = _proj_res(a_s, w_down, hs, norm_ffn_post[layer], g_next, bsz, FFN_TK, w_up=w_up,
                            layer=layer, name="ffn")

    def stacked(kind):
        return tuple(jnp.stack(parts) for parts in zip(*collected[kind]))

    return ((hp.reshape(batch, seq, d_model), hs.reshape(bsz, 1, d_model))
            + stacked(0) + stacked(1) + stacked(2))
```

```python
import functools
import math

import jax
import jax.numpy as jnp
from jax import lax
from jax.experimental import pallas as pl
from jax.experimental.pallas import tpu as pltpu

F32 = jnp.float32
BF16 = jnp.bfloat16

RMS_EPS = 1e-6
ROPE_THETA = 10000.0
NEG_BIG = -1e30

LANES = 128
VMEM_LIMIT_BYTES = 56 * 1024 * 1024

SWA_HEAD_DIM = 64
SWA_GROUP = 8
SWA_WINDOW = 128
GDN_HEAD_DIM = 128
GDN_V_PER_QK = 2
GDN_CONV = 4
GDN_CHUNK = 64
MLA_HEADS = 16
MLA_Q_LORA = 512
MLA_KV_LORA = 512
MLA_NOPE = 128
MLA_ROPE = 64
MLA_V = 128


def _params(*sem):
    return pltpu.CompilerParams(dimension_semantics=sem, vmem_limit_bytes=VMEM_LIMIT_BYTES)


def _rms(y, g):
    return y * lax.rsqrt(jnp.mean(y * y, axis=-1, keepdims=True) + RMS_EPS) * g


def _dot(a, b):
    return jnp.dot(a, b, preferred_element_type=F32)


def _dot_nt(a, b):
    return lax.dot_general(a, b, (((1,), (1,)), ((), ())), preferred_element_type=F32)


def _dot_tn(a, b):
    return lax.dot_general(a, b, (((0,), (0,)), ((), ())), preferred_element_type=F32)


def _bmm(a, b):
    return jnp.einsum("nmk,nkp->nmp", a, b, preferred_element_type=F32)


def _bmm_nt(a, b):
    return jnp.einsum("nmk,npk->nmp", a, b, preferred_element_type=F32)


def _sigmoid(x):
    return 1.0 / (1.0 + jnp.exp(-x))


def _silu(x):
    return x * _sigmoid(x)


def _softplus(x):
    return jnp.maximum(x, 0.0) + jnp.log(1.0 + jnp.exp(-jnp.abs(x)))


def _rope_block(y, cos, sin_signed):
    lane = lax.broadcasted_iota(jnp.int32, (1, LANES), 1)
    first = (lane % 64) < 32
    swapped = jnp.where(first, pltpu.roll(y, 96, 1), pltpu.roll(y, 32, 1))
    return y * cos + swapped * sin_signed


def _rope_tables(pos):
    half = SWA_HEAD_DIM // 2
    inv_freq = ROPE_THETA ** (-jnp.arange(half, dtype=F32) * (2.0 / SWA_HEAD_DIM))
    ang = pos.astype(F32)[:, None] * inv_freq[None, :]
    cos, sin = jnp.cos(ang), jnp.sin(ang)
    return jnp.tile(cos, (1, 4)), jnp.tile(jnp.concatenate([-sin, sin], axis=1), (1, 2))


def _rmsnorm_kernel(x_ref, g_ref, o_ref):
    o_ref[...] = _rms(x_ref[...], g_ref[...]).astype(o_ref.dtype)


def _rmsnorm(x, g, tm):
    m, d = x.shape
    return pl.pallas_call(
        _rmsnorm_kernel,
        grid=(m // tm,),
        in_specs=[pl.BlockSpec((tm, d), lambda i: (i, 0)), pl.BlockSpec((1, d), lambda i: (0, 0))],
        out_specs=pl.BlockSpec((tm, d), lambda i: (i, 0)),
        out_shape=jax.ShapeDtypeStruct((m, d), BF16),
        compiler_params=_params("parallel"),
        name="rmsnorm",
    )(x, g.reshape(1, d))


def _mm_kernel(*refs, has_bias, rope_cols, tn):
    a_ref, w_ref = refs[0], refs[1]
    idx = 2
    if has_bias:
        b_ref = refs[idx]
        idx += 1
    if rope_cols is not None:
        cos_ref, sin_ref = refs[idx], refs[idx + 1]
        idx += 2
    o_ref = refs[idx]
    y = _dot(a_ref[...], w_ref[...])
    if has_bias:
        y = y + b_ref[...]
    if rope_cols is None:
        o_ref[...] = y.astype(o_ref.dtype)
        return
    j = pl.program_id(1)
    cos, sin = cos_ref[...], sin_ref[...]
    for c in range(tn // LANES):
        col0 = j * tn + c * LANES
        on = (col0 >= rope_cols[0]) & (col0 < rope_cols[1])
        yc = y[:, c * LANES:(c + 1) * LANES]
        o_ref[:, c * LANES:(c + 1) * LANES] = jnp.where(on, _rope_block(yc, cos, sin), yc).astype(o_ref.dtype)


def _matmul(a, w, tm, tn, bias=None, rope=None, rope_cols=None, out_dtype=BF16, name="matmul"):
    m, k = a.shape
    n = w.shape[1]
    assert m % tm == 0 and n % tn == 0, (m, tm, n, tn)
    in_specs = [pl.BlockSpec((tm, k), lambda i, j: (i, 0)), pl.BlockSpec((k, tn), lambda i, j: (0, j))]
    args = [a, w]
    if bias is not None:
        in_specs.append(pl.BlockSpec((1, tn), lambda i, j: (0, j)))
        args.append(bias.reshape(1, n).astype(F32))
    if rope is not None:
        rope_blocks = rope[0].shape[0] // tm
        in_specs += [pl.BlockSpec((tm, LANES), lambda i, j: (i % rope_blocks, 0))] * 2
        args += list(rope)
    return pl.pallas_call(
        functools.partial(_mm_kernel, has_bias=bias is not None, rope_cols=rope_cols, tn=tn),
        grid=(m // tm, n // tn),
        in_specs=in_specs,
        out_specs=pl.BlockSpec((tm, tn), lambda i, j: (i, j)),
        out_shape=jax.ShapeDtypeStruct((m, n), out_dtype),
        compiler_params=_params("parallel", "arbitrary"),
        name=name,
    )(*args)


PROJ_PIECE = 512


def _proj_res_kernel(*refs, has_up, has_next, nk):
    if has_up:
        x_ref, wu_ref, w_ref, h_ref, gp_ref = refs[:5]
        idx = 5
    else:
        x_ref, w_ref, h_ref, gp_ref = refs[:4]
        idx = 4
    if has_next:
        gn_ref = refs[idx]
        idx += 1
    ho_ref = refs[idx]
    idx += 1
    if has_next:
        ao_ref = refs[idx]
        idx += 1
    acc_ref = refs[idx] if nk > 1 else None

    def finish(y):
        hn = h_ref[...] + _rms(y, gp_ref[...])
        ho_ref[...] = hn
        if has_next:
            ao_ref[...] = _rms(hn, gn_ref[...]).astype(BF16)

    if has_up:
        tk = wu_ref.shape[1]
        piece = min(tk, PROJ_PIECE)
        ups = []
        for p in range(tk // piece):
            u = jnp.maximum(_dot(x_ref[...], wu_ref[:, p * piece:(p + 1) * piece]), 0.0)
            ups.append((u * u).astype(BF16))
        x = jnp.concatenate(ups, axis=1) if len(ups) > 1 else ups[0]
    else:
        x = x_ref[...]

    if nk == 1:
        finish(_dot(x, w_ref[...]))
        return
    k = pl.program_id(1)

    @pl.when(k == 0)
    def _():
        acc_ref[...] = jnp.zeros_like(acc_ref)

    d = w_ref.shape[1]
    piece = min(d, PROJ_PIECE)
    for n in range(d // piece):
        cols = slice(n * piece, (n + 1) * piece)
        acc_ref[:, cols] += _dot(x, w_ref[:, cols])

    @pl.when(k == nk - 1)
    def _():
        finish(acc_ref[...])


def _proj_res(x, w, h, g_post, g_next, tm, tk, w_up=None, layer=0, name="proj_res"):
    m, d = h.shape
    kdim = w.shape[1]
    nk = kdim // tk
    has_up, has_next = w_up is not None, g_next is not None
    if has_up:
        dk = x.shape[1]
        in_specs = [pl.BlockSpec((tm, dk), lambda i, k: (i, 0)),
                    pl.BlockSpec((None, dk, tk), lambda i, k: (layer, 0, k)),
                    pl.BlockSpec((None, tk, d), lambda i, k: (layer, k, 0))]
        args = [x, w_up, w]
    else:
        in_specs = [pl.BlockSpec((tm, tk), lambda i, k: (i, k)),
                    pl.BlockSpec((None, tk, d), lambda i, k: (layer, k, 0))]
        args = [x, w]
    row = pl.BlockSpec((tm, d), lambda i, k: (i, 0))
    vec = pl.BlockSpec((1, d), lambda i, k: (0, 0))
    in_specs += [row, vec]
    args += [h, g_post.reshape(1, d)]
    out_specs, out_shape = [row], [jax.ShapeDtypeStruct((m, d), F32)]
    if has_next:
        in_specs.append(vec)
        args.append(g_next.reshape(1, d))
        out_specs.append(row)
        out_shape.append(jax.ShapeDtypeStruct((m, d), BF16))
    outs = pl.pallas_call(
        functools.partial(_proj_res_kernel, has_up=has_up, has_next=has_next, nk=nk),
        grid=(m // tm, nk),
        in_specs=in_specs,
        out_specs=out_specs,
        out_shape=out_shape,
        scratch_shapes=[pltpu.VMEM((tm, d), F32)] if nk > 1 else [],
        compiler_params=_params("parallel", "arbitrary"),
        name=name,
    )(*args)
    return (outs[0], outs[1]) if has_next else (outs[0], None)


def _softmax_sink_unnorm(s, sink):
    m = jnp.maximum(jnp.max(s, axis=-1, keepdims=True), sink)
    p = jnp.exp(s - m)
    return p, 1.0 / (jnp.sum(p, axis=-1, keepdims=True) + jnp.exp(sink - m))


def _swa_prompt_kernel(sink_ref, q_ref, kp_ref, kc_ref, vp_ref, vc_ref, o_ref, *, nb):
    w = SWA_WINDOW
    n = pl.program_id(0) % nb
    kvh = pl.program_id(1)
    odd = (kvh % 2) == 1
    lane = lax.broadcasted_iota(jnp.int32, (1, LANES), 1)
    mine = (lane >= 64) == odd

    def pair_operand(p_ref, c_ref):
        blk = jnp.concatenate([p_ref[...], c_ref[...]], axis=0).astype(F32)
        own = jnp.where(mine, blk, 0.0)
        other = pltpu.roll(own, 64, 1)
        lo = jnp.where(odd, other, own)
        hi = jnp.where(odd, own, other)
        return jnp.concatenate([lo, hi], axis=0).astype(BF16)

    k2 = pair_operand(kp_ref, kc_ref)
    v2 = pair_operand(vp_ref, vc_ref)
    qi = lax.broadcasted_iota(jnp.int32, (w, 2 * w), 0)
    kj = lax.broadcasted_iota(jnp.int32, (w, 2 * w), 1)
    rel = qi + w - kj
    mask = (rel >= 0) & (rel < w) & ((n > 0) | (kj >= w))
    for p in range(SWA_GROUP // 2):
        s2 = _dot_nt(q_ref[:, p * LANES:(p + 1) * LANES], k2)
        ps, rs = [], []
        for e in range(2):
            s = jnp.where(mask, s2[:, e * 2 * w:(e + 1) * 2 * w], NEG_BIG)
            pe, re = _softmax_sink_unnorm(s, sink_ref[kvh * SWA_GROUP + 2 * p + e])
            ps.append(pe.astype(BF16))
            rs.append(re)
        o2 = _dot(jnp.concatenate(ps, axis=1), v2)
        o_ref[:, p * LANES:(p + 1) * LANES] = (o2 * jnp.where(lane < 64, rs[0], rs[1])).astype(o_ref.dtype)


def _swa_prompt_attn(qkv, sinks, batch, seq):
    w = SWA_WINDOW
    nb = seq // w
    n_heads = sinks.shape[0]
    kvh = n_heads // SWA_GROUP
    q_dim = n_heads * SWA_HEAD_DIM
    kblk0 = q_dim // LANES
    vblk0 = (q_dim + kvh * SWA_HEAD_DIM) // LANES

    def prev(i):
        return jnp.where(i % nb == 0, i, i - 1)

    return pl.pallas_call(
        functools.partial(_swa_prompt_kernel, nb=nb),
        grid=(batch * nb, kvh),
        in_specs=[
            pl.BlockSpec(memory_space=pltpu.SMEM),
            pl.BlockSpec((w, SWA_GROUP * SWA_HEAD_DIM), lambda i, h: (i, h)),
            pl.BlockSpec((w, LANES), lambda i, h: (prev(i), kblk0 + h // 2)),
            pl.BlockSpec((w, LANES), lambda i, h: (i, kblk0 + h // 2)),
            pl.BlockSpec((w, LANES), lambda i, h: (prev(i), vblk0 + h // 2)),
            pl.BlockSpec((w, LANES), lambda i, h: (i, vblk0 + h // 2)),
        ],
        out_specs=pl.BlockSpec((w, SWA_GROUP * SWA_HEAD_DIM), lambda i, h: (i, h)),
        out_shape=jax.ShapeDtypeStruct((batch * seq, q_dim), BF16),
        compiler_params=_params("parallel", "arbitrary"),
        name="swa_prompt_attn",
    )(sinks.astype(F32), qkv, qkv, qkv, qkv, qkv)


def _swa_sample_kernel(q_ref, kn_ref, vn_ref, ck_ref, cv_ref, sink_ref, o_ref, cko_ref, cvo_ref, *, bt, npair):
    p_rows = ck_ref.shape[1]
    lane = lax.broadcasted_iota(jnp.int32, (1, LANES), 1)
    row = lax.broadcasted_iota(jnp.int32, (p_rows, 1), 0)
    grp = lax.broadcasted_iota(jnp.int32, (2 * SWA_GROUP, 1), 0)
    for b in range(bt):
        k_new, v_new = kn_ref[b], vn_ref[b]
        cko_ref[b, pl.ds(0, p_rows - 1), :] = ck_ref[b, pl.ds(1, p_rows - 1), :]
        cvo_ref[b, pl.ds(0, p_rows - 1), :] = cv_ref[b, pl.ds(1, p_rows - 1), :]
        cko_ref[b, pl.ds(p_rows - 1, 1), :] = k_new
        cvo_ref[b, pl.ds(p_rows - 1, 1), :] = v_new
        for c in range(npair):
            sl = slice(c * LANES, (c + 1) * LANES)
            kk = jnp.where(row == 0, k_new[:, sl], ck_ref[b, :, sl]).astype(BF16)
            vv = jnp.where(row == 0, v_new[:, sl], cv_ref[b, :, sl]).astype(BF16)
            q2 = q_ref[b, pl.ds(c * 2 * SWA_GROUP, 2 * SWA_GROUP), :]
            q2 = jnp.where((lane >= 64) == (grp >= SWA_GROUP), q2, 0.0).astype(BF16)
            s = _dot_nt(q2, kk)
            pe, re = _softmax_sink_unnorm(s, sink_ref[c])
            o2 = _dot(pe.astype(BF16), vv) * re
            o_ref[b, pl.ds(c * SWA_GROUP, SWA_GROUP), :] = jnp.where(
                lane < 64, o2[:SWA_GROUP], o2[SWA_GROUP:]).astype(o_ref.dtype)


def _swa_sample_attn(qkv, ck, cv, sinks):
    bsz, p_rows, kvh, hd = ck.shape
    assert p_rows == SWA_WINDOW and hd == SWA_HEAD_DIM
    n_heads = kvh * SWA_GROUP
    q_dim, kv_dim = n_heads * hd, kvh * hd
    npair = kvh // 2
    bt = 8
    q = qkv[:, :q_dim].reshape(bsz, npair, 2, SWA_GROUP, hd).transpose(0, 1, 3, 2, 4)
    q = jnp.broadcast_to(q.reshape(bsz, npair, 1, SWA_GROUP, 2 * hd), (bsz, npair, 2, SWA_GROUP, 2 * hd))
    q = q.reshape(bsz, npair * 2 * SWA_GROUP, 2 * hd)
    k_new = qkv[:, q_dim:q_dim + kv_dim].reshape(bsz, 1, kv_dim)
    v_new = qkv[:, q_dim + kv_dim:].reshape(bsz, 1, kv_dim)
    sink = sinks.astype(F32).reshape(npair, 2 * SWA_GROUP, 1)
    o, cko, cvo = pl.pallas_call(
        functools.partial(_swa_sample_kernel, bt=bt, npair=npair),
        grid=(bsz // bt,),
        in_specs=[
            pl.BlockSpec((bt, npair * 2 * SWA_GROUP, LANES), lambda i: (i, 0, 0)),
            pl.BlockSpec((bt, 1, kv_dim), lambda i: (i, 0, 0)),
            pl.BlockSpec((bt, 1, kv_dim), lambda i: (i, 0, 0)),
            pl.BlockSpec((bt, p_rows, kv_dim), lambda i: (i, 0, 0)),
            pl.BlockSpec((bt, p_rows, kv_dim), lambda i: (i, 0, 0)),
            pl.BlockSpec((npair, 2 * SWA_GROUP, 1), lambda i: (0, 0, 0)),
        ],
        out_specs=[
            pl.BlockSpec((bt, npair * SWA_GROUP, LANES), lambda i: (i, 0, 0)),
            pl.BlockSpec((bt, p_rows, kv_dim), lambda i: (i, 0, 0)),
            pl.BlockSpec((bt, p_rows, kv_dim), lambda i: (i, 0, 0)),
        ],
        out_shape=[
            jax.ShapeDtypeStruct((bsz, npair * SWA_GROUP, LANES), BF16),
            jax.ShapeDtypeStruct((bsz, p_rows, kv_dim), F32),
            jax.ShapeDtypeStruct((bsz, p_rows, kv_dim), F32),
        ],
        compiler_params=_params("parallel"),
        name="swa_sample_attn",
    )(q, k_new, v_new, ck.reshape(bsz, p_rows, kv_dim), cv.reshape(bsz, p_rows, kv_dim), sink)
    o = o.reshape(bsz, npair, SWA_GROUP, 2, hd).transpose(0, 1, 3, 2, 4).reshape(bsz, q_dim)
    return o, cko.reshape(bsz, p_rows, kvh, hd), cvo.reshape(bsz, p_rows, kvh, hd)


def _gdn_gates_kernel(ba_ref, alog_ref, dt_ref, beta_ref, g_ref, gc_ref, *, chunk, nh):
    rows = ba_ref.shape[0]
    x = ba_ref[...]
    beta_ref[...] = _sigmoid(x)
    a = pltpu.roll(x, LANES - nh, 1)
    g = -jnp.exp(alog_ref[...]) * _softplus(a + dt_ref[...])
    g_ref[...] = g
    pos = lax.broadcasted_iota(jnp.int32, (rows, 1), 0) % chunk
    acc = g
    shift = 1
    while shift < chunk:
        acc = acc + jnp.where(pos >= shift, pltpu.roll(acc, shift, 0), 0.0)
        shift *= 2
    gc_ref[...] = acc


def _gdn_gates(ba, a_log, dt_bias, tr, chunk):
    m = ba.shape[0]
    nh = a_log.shape[0]
    row = pl.BlockSpec((tr, LANES), lambda i: (i, 0))
    vec = pl.BlockSpec((1, LANES), lambda i: (0, 0))

    def pad(x):
        return jnp.pad(x.astype(F32), (0, LANES - nh)).reshape(1, LANES)

    return pl.pallas_call(
        functools.partial(_gdn_gates_kernel, chunk=chunk, nh=nh),
        grid=(m // tr,),
        in_specs=[row, vec, vec],
        out_specs=[row, row, row],
        out_shape=[jax.ShapeDtypeStruct((m, LANES), F32)] * 3,
        compiler_params=_params("parallel"),
        name="gdn_gates",
    )(ba, pad(a_log), pad(dt_bias))


def _gdn_conv_kernel(x_ref, halo_ref, w_ref, o_ref, ext_ref, *, seq, key_dim, scale):
    tr, tc = x_ref.shape
    i, j = pl.program_id(0), pl.program_id(1)
    hb = halo_ref.shape[0]
    starts_seq = (i * tr) % seq == 0
    ext_ref[pl.ds(0, 8), :] = jnp.where(starts_seq, 0.0, halo_ref[pl.ds(hb - 8, 8), :].astype(F32))
    ext_ref[pl.ds(8, tr), :] = x_ref[...].astype(F32)
    acc = ext_ref[pl.ds(8 - (GDN_CONV - 1), tr), :] * w_ref[0:1, :]
    for t in range(1, GDN_CONV):
        acc = acc + ext_ref[pl.ds(8 - (GDN_CONV - 1) + t, tr), :] * w_ref[t:t + 1, :]
    c = _silu(acc)
    col0 = j * tc
    is_q = col0 < key_dim
    is_qk = col0 < 2 * key_dim
    for hblk in range(tc // LANES):
        ch = c[:, hblk * LANES:(hblk + 1) * LANES]
        nrm = ch * lax.rsqrt(jnp.sum(ch * ch, axis=-1, keepdims=True) + RMS_EPS)
        nrm = jnp.where(is_q, nrm * scale, nrm)
        o_ref[:, hblk * LANES:(hblk + 1) * LANES] = jnp.where(is_qk, nrm, ch).astype(o_ref.dtype)


def _gdn_conv(proj, conv_w, seq, key_dim, conv_dim, tr, tc):
    m = proj.shape[0]
    hb = 16
    return pl.pallas_call(
        functools.partial(_gdn_conv_kernel, seq=seq, key_dim=key_dim, scale=GDN_HEAD_DIM ** -0.5),
        grid=(m // tr, conv_dim // tc),
        in_specs=[
            pl.BlockSpec((tr, tc), lambda i, j: (i, j)),
            pl.BlockSpec((hb, tc), lambda i, j: (jnp.maximum(i * (tr // hb) - 1, 0), j)),
            pl.BlockSpec((GDN_CONV, tc), lambda i, j: (0, j)),
        ],
        out_specs=pl.BlockSpec((tr, tc), lambda i, j: (i, j)),
        out_shape=jax.ShapeDtypeStruct((m, conv_dim), BF16),
        scratch_shapes=[pltpu.VMEM((tr + 8, tc), F32)],
        compiler_params=_params("parallel", "arbitrary"),
        name="gdn_conv",
    )(proj, proj, conv_w.astype(F32))


GDN_INV_BASE = 4
GDN_QK_PER_STEP = 8


def _unit_lower_inverse(a):
    n, c, _ = a.shape
    base = GDN_INV_BASE
    ri = lax.broadcasted_iota(jnp.int32, (1, c, c), 1)
    ci = lax.broadcasted_iota(jnp.int32, (1, c, c), 2)
    x = jnp.broadcast_to((ri == ci).astype(F32), (n, c, c))
    diag_blocks = jnp.where(ri // base == ci // base, a, 0.0)
    col_in_block = ci % base
    for j in range(base - 1):
        coef = jnp.sum(jnp.where(col_in_block == j, diag_blocks, 0.0), axis=2, keepdims=True)
        rows = jnp.concatenate(
            [jnp.broadcast_to(x[:, b * base + j:b * base + j + 1, :], (n, base, c)) for b in range(c // base)], axis=1)
        x = x - coef * rows
    size = base
    while size < c:
        off = jnp.where((ri // (2 * size) == ci // (2 * size)) & (ri // size != ci // size), a, 0.0)
        xb = x.astype(BF16)
        x = x - _bmm(_bmm(xb, off.astype(BF16)).astype(BF16), xb)
        size *= 2
    return x


def _gdn_scan_kernel(q_ref, k_ref, v_ref, z_ref, gc_ref, beta_ref, gt_ref, nw_ref, y_ref, so_ref, s_ref, *, chunk):
    rows = q_ref.shape[0]
    hd = GDN_HEAD_DIM
    nc = rows // chunk
    n_qk, per = GDN_QK_PER_STEP, GDN_V_PER_QK
    n_v = n_qk * per
    hq0, t = pl.program_id(1) * n_qk, pl.program_id(2)

    @pl.when(t == 0)
    def _():
        s_ref[...] = jnp.zeros_like(s_ref)

    lane = lax.broadcasted_iota(jnp.int32, (1, LANES), 1)
    ri = lax.broadcasted_iota(jnp.int32, (1, chunk, chunk), 1)
    ci = lax.broadcasted_iota(jnp.int32, (1, chunk, chunk), 2)
    tri, strict = ri >= ci, ri > ci
    q3 = [q_ref[:, h * hd:(h + 1) * hd].reshape(nc, chunk, hd) for h in range(n_qk)]
    k3 = [k_ref[:, h * hd:(h + 1) * hd].reshape(nc, chunk, hd) for h in range(n_qk)]
    gkk = [_bmm_nt(k3[h], k3[h]) for h in range(n_qk)]
    gqk = [_bmm_nt(q3[h], k3[h]) for h in range(n_qk)]
    bcol, eg, w_last, eg_last, lower, qk = [], [], [], [], [], []
    for i in range(n_v):
        pick = lane == hq0 * per + i
        gcol = jnp.sum(jnp.where(pick, gc_ref[...], 0.0), axis=1, keepdims=True).reshape(nc, chunk, 1)
        bcol.append(jnp.sum(jnp.where(pick, beta_ref[...], 0.0), axis=1, keepdims=True).reshape(nc, chunk, 1))
        grow_full = gt_ref[pl.ds(hq0 * per + i, 1), :]
        grow = jnp.stack([grow_full[:, c * chunk:(c + 1) * chunk] for c in range(nc)], axis=0)
        decay = jnp.where(tri, jnp.exp(jnp.where(tri, gcol - grow, 0.0)), 0.0)
        lower.append(jnp.where(strict, bcol[i] * gkk[i // per] * decay, 0.0))
        qk.append((gqk[i // per] * decay).astype(BF16))
        g_last = gcol[:, chunk - 1:chunk, :]
        eg.append(jnp.exp(gcol))
        w_last.append(jnp.exp(g_last - gcol))
        eg_last.append(jnp.exp(g_last))
    tinv = _unit_lower_inverse(jnp.concatenate(lower, axis=0)).astype(BF16)

    state = [s_ref[h] for h in range(n_qk)]
    for c in range(nc):
        r0 = c * chunk
        sb = [state[h].astype(BF16) for h in range(n_qk)]
        ks = [_dot(k3[h][c], sb[h]) for h in range(n_qk)]
        qs = [_dot(q3[h][c], sb[h]) for h in range(n_qk)]
        rhs = []
        for i in range(n_v):
            h, e = divmod(i, per)
            v = v_ref[pl.ds(r0, chunk), pl.ds(i * hd, hd)].astype(F32)
            rhs.append((bcol[i][c] * (v - eg[i][c] * ks[h][:, e * hd:(e + 1) * hd])).astype(BF16))
        v_new = _bmm(jnp.stack([tinv[i * nc + c] for i in range(n_v)]), jnp.stack(rhs))
        o_intra = _bmm(jnp.stack([qk[i][c] for i in range(n_v)]), v_new.astype(BF16))
        for i in range(n_v):
            h, e = divmod(i, per)
            o = eg[i][c] * qs[h][:, e * hd:(e + 1) * hd] + o_intra[i]
            gate = _silu(z_ref[pl.ds(r0, chunk), pl.ds(i * hd, hd)].astype(F32))
            y_ref[pl.ds(r0, chunk), pl.ds(i * hd, hd)] = (_rms(o, nw_ref[...]) * gate).astype(y_ref.dtype)
        for h in range(n_qk):
            mine = range(h * per, (h + 1) * per)
            wv = jnp.concatenate([(w_last[i][c] * v_new[i]).astype(BF16) for i in mine], axis=1)
            keep = jnp.concatenate([jnp.broadcast_to(eg_last[i][c], (1, hd)) for i in mine], axis=1)
            state[h] = keep * state[h] + _dot_tn(k3[h][c], wv)
    for h in range(n_qk):
        s_ref[h] = state[h]

    @pl.when(t == pl.num_programs(2) - 1)
    def _():
        for i in range(n_v):
            h, e = divmod(i, per)
            so_ref[0, i] = state[h][:, e * hd:(e + 1) * hd]


def _gdn_scan(qkv, proj, gc, beta, gc_t, norm_w, batch, seq, n_qk, rows, chunk):
    hd = GDN_HEAD_DIM
    m = batch * seq
    nt = seq // rows
    n_v = n_qk * GDN_V_PER_QK
    qw = GDN_QK_PER_STEP * hd
    n_local = GDN_QK_PER_STEP * GDN_V_PER_QK
    vw = n_local * hd
    k_blk0 = n_qk * hd // qw
    v_blk0 = 2 * n_qk * hd // vw
    z_blk0 = (2 * n_qk * hd + n_v * hd) // vw
    y, s_fin = pl.pallas_call(
        functools.partial(_gdn_scan_kernel, chunk=chunk),
        grid=(batch, n_qk // GDN_QK_PER_STEP, nt),
        in_specs=[
            pl.BlockSpec((rows, qw), lambda b, h, t: (b * nt + t, h)),
            pl.BlockSpec((rows, qw), lambda b, h, t: (b * nt + t, k_blk0 + h)),
            pl.BlockSpec((rows, vw), lambda b, h, t: (b * nt + t, v_blk0 + h)),
            pl.BlockSpec((rows, vw), lambda b, h, t: (b * nt + t, z_blk0 + h)),
            pl.BlockSpec((rows, LANES), lambda b, h, t: (b * nt + t, 0)),
            pl.BlockSpec((rows, LANES), lambda b, h, t: (b * nt + t, 0)),
            pl.BlockSpec((n_v, rows), lambda b, h, t: (0, b * nt + t)),
            pl.BlockSpec((1, hd), lambda b, h, t: (0, 0)),
        ],
        out_specs=[
            pl.BlockSpec((rows, vw), lambda b, h, t: (b * nt + t, h)),
            pl.BlockSpec((1, n_local, hd, hd), lambda b, h, t: (b, h, 0, 0)),
        ],
        out_shape=[
            jax.ShapeDtypeStruct((m, n_v * hd), BF16),
            jax.ShapeDtypeStruct((batch, n_v, hd, hd), F32),
        ],
        scratch_shapes=[pltpu.VMEM((GDN_QK_PER_STEP, hd, GDN_V_PER_QK * hd), F32)],
        compiler_params=_params("parallel", "parallel", "arbitrary"),
        name="gdn_scan",
    )(qkv, qkv, qkv, proj, gc, beta, gc_t, norm_w.reshape(1, hd).astype(F32))
    return y, s_fin


def _gdn_conv_step_kernel(s0_ref, s1_ref, s2_ref, x_ref, w_ref, o_ref, *, key_dim, scale):
    tc = x_ref.shape[1]
    acc = s0_ref[...] * w_ref[0:1, :] + s1_ref[...] * w_ref[1:2, :] + s2_ref[...] * w_ref[2:3, :] \
        + x_ref[...] * w_ref[3:4, :]
    c = _silu(acc)
    col0 = pl.program_id(0) * tc
    is_q = col0 < key_dim
    is_qk = col0 < 2 * key_dim
    for hblk in range(tc // LANES):
        ch = c[:, hblk * LANES:(hblk + 1) * LANES]
        nrm = ch * lax.rsqrt(jnp.sum(ch * ch, axis=-1, keepdims=True) + RMS_EPS)
        nrm = jnp.where(is_q, nrm * scale, nrm)
        o_ref[:, hblk * LANES:(hblk + 1) * LANES] = jnp.where(is_qk, nrm, ch)


def _gdn_conv_step(conv_state, mixed, conv_w, key_dim, tc):
    bsz, taps, cdim = conv_state.shape
    assert taps == GDN_CONV - 1
    ncol = cdim // tc
    flat = conv_state.reshape(bsz, taps * cdim)
    return pl.pallas_call(
        functools.partial(_gdn_conv_step_kernel, key_dim=key_dim, scale=GDN_HEAD_DIM ** -0.5),
        grid=(ncol,),
        in_specs=[
            pl.BlockSpec((bsz, tc), lambda j: (0, j)),
            pl.BlockSpec((bsz, tc), lambda j: (0, ncol + j)),
            pl.BlockSpec((bsz, tc), lambda j: (0, 2 * ncol + j)),
            pl.BlockSpec((bsz, tc), lambda j: (0, j)),
            pl.BlockSpec((GDN_CONV, tc), lambda j: (0, j)),
        ],
        out_specs=pl.BlockSpec((bsz, tc), lambda j: (0, j)),
        out_shape=jax.ShapeDtypeStruct((bsz, cdim), F32),
        compiler_params=_params("parallel"),
        name="gdn_conv_step",
    )(flat, flat, flat, mixed, conv_w.astype(F32))


def _gdn_step_kernel(q_ref, k_ref, v_ref, z_ref, g_ref, beta_ref, nw_ref, s_ref, y_ref, so_ref, o_scr):
    bsz = q_ref.shape[0]
    hv = pl.program_id(0)
    pick = lax.broadcasted_iota(jnp.int32, (1, LANES), 1) == hv
    egcol = jnp.exp(jnp.sum(jnp.where(pick, g_ref[...], 0.0), axis=1, keepdims=True))
    bcol = jnp.sum(jnp.where(pick, beta_ref[...], 0.0), axis=1, keepdims=True)
    q_t = q_ref[...].T
    k_t = k_ref[...].T
    for b in range(bsz):
        kc, qc = k_t[:, b:b + 1], q_t[:, b:b + 1]
        s1 = s_ref[b, 0] * egcol[b:b + 1, :]
        delta = (v_ref[b:b + 1, :] - jnp.sum(s1 * kc, axis=0, keepdims=True)) * bcol[b:b + 1, :]
        s2 = s1 + kc * delta
        so_ref[b, 0] = s2
        o_scr[b:b + 1, :] = jnp.sum(s2 * qc, axis=0, keepdims=True)
    y_ref[...] = (_rms(o_scr[...], nw_ref[...]) * _silu(z_ref[...])).astype(y_ref.dtype)


def _gdn_step(qkv, proj, g, beta, norm_w, ssm, n_qk):
    bsz, n_v, hd, _ = ssm.shape
    k_blk0, v_blk0 = n_qk, 2 * n_qk
    z_blk0 = 2 * n_qk + n_v
    row = pl.BlockSpec((bsz, LANES), lambda h: (0, 0))
    st = pl.BlockSpec((bsz, 1, hd, hd), lambda h: (0, h, 0, 0))
    return pl.pallas_call(
        _gdn_step_kernel,
        grid=(n_v,),
        in_specs=[
            pl.BlockSpec((bsz, hd), lambda h: (0, h // GDN_V_PER_QK)),
            pl.BlockSpec((bsz, hd), lambda h: (0, k_blk0 + h // GDN_V_PER_QK)),
            pl.BlockSpec((bsz, hd), lambda h: (0, v_blk0 + h)),
            pl.BlockSpec((bsz, hd), lambda h: (0, z_blk0 + h)),
            row, row,
            pl.BlockSpec((1, hd), lambda h: (0, 0)),
            st,
        ],
        out_specs=[pl.BlockSpec((bsz, hd), lambda h: (0, h)), st],
        out_shape=[jax.ShapeDtypeStruct((bsz, n_v * hd), BF16), jax.ShapeDtypeStruct(ssm.shape, F32)],
        scratch_shapes=[pltpu.VMEM((bsz, hd), F32)],
        compiler_params=_params("parallel"),
        name="gdn_step",
    )(qkv, qkv, qkv, proj, g, beta, norm_w.reshape(1, hd).astype(F32), ssm)


MLA_IN_COLS = MLA_Q_LORA + MLA_KV_LORA + 2 * LANES
MLA_Q_COLS = MLA_HEADS * (MLA_NOPE + MLA_ROPE)
MLA_QK_W = MLA_NOPE + LANES
MLA_KV_W = MLA_QK_W + MLA_V
LOG2E = math.log2(math.e)


def _mla_proj_kernel(*refs, has_kv):
    a_ref, win_ref, qn_ref, kvn_ref, wuq_ref = refs[:5]
    idx = 5
    if has_kv:
        wukv_ref = refs[idx]
        idx += 1
    cos_ref, sin_ref, q_ref = refs[idx:idx + 3]
    idx += 3
    if has_kv:
        kv_ref = refs[idx]
        idx += 1
    ckv_ref, kr_ref = refs[idx:idx + 2]
    cos, sin = cos_ref[...], sin_ref[...]
    p = _dot(a_ref[...], win_ref[...])
    cq = _rms(p[:, :MLA_Q_LORA], qn_ref[...]).astype(BF16)
    ckv = _rms(p[:, MLA_Q_LORA:MLA_Q_LORA + MLA_KV_LORA], kvn_ref[...])
    ckv_ref[...] = ckv
    r0 = MLA_Q_LORA + MLA_KV_LORA
    kr_ref[...] = _rope_block(p[:, r0:r0 + LANES], cos, sin)
    q = _dot(cq, wuq_ref[...])
    nope = MLA_HEADS * MLA_NOPE
    q_rope = [_rope_block(q[:, nope + c * LANES:nope + (c + 1) * LANES], cos, sin).astype(q_ref.dtype)
              for c in range(MLA_HEADS * MLA_ROPE // LANES)]
    if not has_kv:
        q_ref[:, :nope] = q[:, :nope].astype(q_ref.dtype)
        for c, qr in enumerate(q_rope):
            q_ref[:, nope + c * LANES:nope + (c + 1) * LANES] = qr
        return
    k_rope = [kr_ref[...].astype(kv_ref.dtype),
              _rope_block(p[:, r0 + LANES:r0 + 2 * LANES], cos, sin).astype(kv_ref.dtype)]
    kv = _dot(ckv.astype(BF16), wukv_ref[...])
    hd = MLA_NOPE
    for h in range(MLA_HEADS):
        q_ref[:, MLA_QK_W * h:MLA_QK_W * h + hd] = q[:, hd * h:hd * (h + 1)].astype(q_ref.dtype)
        q_ref[:, MLA_QK_W * h + hd:MLA_QK_W * (h + 1)] = q_rope[h // 2]
        k0 = MLA_KV_W * h
        kv_ref[:, k0:k0 + hd] = kv[:, 2 * hd * h:2 * hd * h + hd].astype(kv_ref.dtype)
        kv_ref[:, k0 + hd:k0 + MLA_QK_W] = k_rope[h % 2]
        kv_ref[:, k0 + MLA_QK_W:k0 + MLA_KV_W] = kv[:, 2 * hd * h + hd:2 * hd * (h + 1)].astype(kv_ref.dtype)


def _mla_project(a, w_in, q_norm, w_uq, kv_norm, rope, tm, w_ukv=None):
    m, d = a.shape
    has_kv = w_ukv is not None
    row = lambda n: pl.BlockSpec((tm, n), lambda i: (i, 0))
    full = lambda s: pl.BlockSpec(s, lambda i: (0, 0))
    in_specs = [row(d), full(w_in.shape), full((1, MLA_Q_LORA)), full((1, MLA_KV_LORA)), full(w_uq.shape)]
    args = [a, w_in, q_norm.reshape(1, -1).astype(F32), kv_norm.reshape(1, -1).astype(F32), w_uq]
    q_cols = MLA_HEADS * MLA_QK_W if has_kv else MLA_Q_COLS
    out_specs, out_shape = [row(q_cols)], [jax.ShapeDtypeStruct((m, q_cols), BF16)]
    if has_kv:
        in_specs.append(full(w_ukv.shape))
        args.append(w_ukv)
        out_specs.append(row(MLA_HEADS * MLA_KV_W))
        out_shape.append(jax.ShapeDtypeStruct((m, MLA_HEADS * MLA_KV_W), BF16))
    rope_blocks = rope[0].shape[0] // tm
    in_specs += [pl.BlockSpec((tm, LANES), lambda i: (i % rope_blocks, 0))] * 2
    args += list(rope)
    out_specs += [row(MLA_KV_LORA), row(LANES)]
    out_shape += [jax.ShapeDtypeStruct((m, MLA_KV_LORA), F32), jax.ShapeDtypeStruct((m, LANES), F32)]
    return pl.pallas_call(
        functools.partial(_mla_proj_kernel, has_kv=has_kv),
        grid=(m // tm,),
        in_specs=in_specs,
        out_specs=out_specs,
        out_shape=out_shape,
        compiler_params=_params("parallel"),
        name="mla_project",
    )(*args)


def _mla_attn_kernel(q_ref, kv_ref, o_ref, *, tq):
    qi = pl.program_id(2)
    causal = (lax.broadcasted_iota(jnp.int32, (tq, tq), 1) <= lax.broadcasted_iota(jnp.int32, (tq, tq), 0))

    def tile(j, carry, on_diagonal):
        r0 = pl.multiple_of(j * tq, tq)
        out = []
        for e in range(2):
            m_prev, l_prev, acc = carry[e]
            k = kv_ref[pl.ds(r0, tq), pl.ds(e * MLA_KV_W, MLA_QK_W)]
            v = kv_ref[pl.ds(r0, tq), pl.ds(e * MLA_KV_W + MLA_QK_W, MLA_V)]
            s = _dot_nt(q_ref[:, e * MLA_QK_W:(e + 1) * MLA_QK_W], k)
            if on_diagonal:
                s = jnp.where(causal, s, NEG_BIG)
            m_new = jnp.maximum(m_prev, jnp.max(s, axis=-1, keepdims=True))
            alpha = jnp.exp2(m_prev - m_new)
            p = jnp.exp2(s - m_new)
            out.append((m_new, alpha * l_prev + jnp.sum(p, axis=-1, keepdims=True),
                        alpha * acc + _dot(p.astype(BF16), v)))
        return tuple(out)

    init = tuple((jnp.full((tq, 1), NEG_BIG, F32), jnp.zeros((tq, 1), F32), jnp.zeros((tq, MLA_V), F32))
                 for _ in range(2))
    carry = lax.fori_loop(0, qi, lambda j, c: tile(j, c, False), init)
    carry = tile(qi, carry, True)
    for e in range(2):
        o_ref[:, e * MLA_V:(e + 1) * MLA_V] = (carry[e][2] / carry[e][1]).astype(o_ref.dtype)


def _mla_prompt_attn(q, kv, batch, seq, tq):
    nq = seq // tq
    return pl.pallas_call(
        functools.partial(_mla_attn_kernel, tq=tq),
        grid=(batch, MLA_HEADS // 2, nq),
        in_specs=[
            pl.BlockSpec((tq, 2 * MLA_QK_W), lambda b, p, qi: (b * nq + qi, p)),
            pl.BlockSpec((seq, 2 * MLA_KV_W), lambda b, p, qi: (b, p)),
        ],
        out_specs=pl.BlockSpec((tq, 2 * MLA_V), lambda b, p, qi: (b * nq + qi, p)),
        out_shape=jax.ShapeDtypeStruct((batch * seq, MLA_HEADS * MLA_V), BF16),
        compiler_params=_params("parallel", "parallel", "arbitrary"),
        name="mla_prompt_attn",
    )(q, kv)


def _head_mm_kernel(x_ref, w_ref, o_ref, *, transpose_w):
    y = _dot_nt(x_ref[...], w_ref[...]) if transpose_w else _dot(x_ref[...], w_ref[...])
    o_ref[...] = y.astype(o_ref.dtype)


def _head_matmul(x, w_ukv, transpose_w):
    bsz = x.shape[0]
    xw = x.shape[1] // MLA_HEADS
    ow = MLA_KV_LORA if transpose_w else MLA_V
    return pl.pallas_call(
        functools.partial(_head_mm_kernel, transpose_w=transpose_w),
        grid=(MLA_HEADS,),
        in_specs=[
            pl.BlockSpec((bsz, xw), lambda h: (0, h)),
            pl.BlockSpec((MLA_KV_LORA, LANES), lambda h: (0, 2 * h if transpose_w else 2 * h + 1)),
        ],
        out_specs=pl.BlockSpec((bsz, ow), lambda h: (0, h)),
        out_shape=jax.ShapeDtypeStruct((bsz, MLA_HEADS * ow), BF16),
        compiler_params=_params("parallel"),
        name="mla_head_matmul",
    )(x, w_ukv)


MLA_DECODE_STREAMS = 2
MLA_DECODE_SLOTS = 3


def _mla_decode_kernel(pt_ref, ql_ref, qr_ref, ckv_ref, krn_ref, lat_hbm, rope_hbm, o_ref,
                       lat_buf, rope_buf, sem, m_ref, l_ref, acc_ref, *, layer, pages):
    b, c = pl.program_id(0), pl.program_id(1)
    n_chunks = pl.num_programs(1)
    n_steps = pl.num_programs(0) * n_chunks
    step = b * n_chunks + c
    n_slots = lat_buf.shape[0]
    ahead = n_slots - 1
    slot = step % n_slots
    page = rope_hbm.shape[3]
    ns = MLA_DECODE_STREAMS
    keys = pages * page // ns

    def page_copies(step_idx, slot_idx):
        bb, cc = step_idx // n_chunks, step_idx % n_chunks
        out = []
        for g in range(pages):
            pg = pt_ref[bb, cc * pages + g]
            out.append(pltpu.make_async_copy(lat_hbm.at[layer, pg], lat_buf.at[slot_idx, pl.ds(g * page, page), :],
                                             sem.at[0, slot_idx]))
            out.append(pltpu.make_async_copy(rope_hbm.at[layer, pg], rope_buf.at[slot_idx, :, pl.ds(g * page, page)],
                                             sem.at[1, slot_idx]))
        return out

    @pl.when(step == 0)
    def _():
        for first in range(ahead):
            for cp in page_copies(first, first):
                cp.start()

    @pl.when(c == 0)
    def _():
        m_ref[...] = jnp.full_like(m_ref, NEG_BIG)
        l_ref[...] = jnp.zeros_like(l_ref)
        acc_ref[...] = jnp.zeros_like(acc_ref)

    @pl.when(step + ahead < n_steps)
    def _():
        for cp in page_copies(step + ahead, (step + ahead) % n_slots):
            cp.start()

    for cp in page_copies(step, slot):
        cp.wait()

    ql, qr = ql_ref[0], qr_ref[0]
    lats, scores = [], []
    for st in range(ns):
        lat = lat_buf[slot, pl.ds(st * keys, keys), :].astype(BF16)
        rope_t = rope_buf[slot, :, pl.ds(st * keys, keys)].astype(BF16)
        lats.append(lat)
        scores.append(_dot_nt(ql, lat) + _dot(qr, rope_t))
    probs = []
    for st in range(ns):
        m_prev = m_ref[st]
        m_new = jnp.maximum(m_prev, jnp.max(scores[st], axis=-1, keepdims=True))
        alpha = jnp.exp2(m_prev - m_new)
        p = jnp.exp2(scores[st] - m_new)
        l_ref[st] = alpha * l_ref[st] + jnp.sum(p, axis=-1, keepdims=True)
        m_ref[st] = m_new
        probs.append((alpha, p.astype(BF16)))
    for st in range(ns):
        acc_ref[st] = probs[st][0] * acc_ref[st] + _dot(probs[st][1], lats[st])

    @pl.when(c == pl.num_programs(1) - 1)
    def _():
        ckv, krn = ckv_ref[0], krn_ref[0]
        s_new = (jnp.sum(ql.astype(F32) * ckv, axis=-1, keepdims=True)
                 + jnp.sum(qr.astype(F32) * krn, axis=-1, keepdims=True))
        m_fin = s_new
        for st in range(ns):
            m_fin = jnp.maximum(m_fin, m_ref[st])
        p_new = jnp.exp2(s_new - m_fin)
        l_fin = p_new
        acc = p_new * ckv
        for st in range(ns):
            w = jnp.exp2(m_ref[st] - m_fin)
            l_fin = l_fin + w * l_ref[st]
            acc = acc + w * acc_ref[st]
        o_ref[0] = (acc / l_fin).astype(o_ref.dtype)


def _mla_decode(q_lat, q_rope, ckv_new, kr_new, lat_cache, rope_cache, page_table, layer, pages):
    bsz, n_pages = page_table.shape
    page = lat_cache.shape[2]
    nh = q_lat.shape[1]
    ns = MLA_DECODE_STREAMS
    assert pages % ns == 0 and n_pages % pages == 0
    assert bsz * (n_pages // pages) >= MLA_DECODE_SLOTS - 1
    rope_cache_t = jnp.swapaxes(rope_cache, 2, 3)

    grid_spec = pltpu.PrefetchScalarGridSpec(
        num_scalar_prefetch=1,
        grid=(bsz, n_pages // pages),
        in_specs=[
            pl.BlockSpec((1, nh, MLA_KV_LORA), lambda b, c, pt: (b, 0, 0)),
            pl.BlockSpec((1, nh, MLA_ROPE), lambda b, c, pt: (b, 0, 0)),
            pl.BlockSpec((1, 1, MLA_KV_LORA), lambda b, c, pt: (b, 0, 0)),
            pl.BlockSpec((1, 1, MLA_ROPE), lambda b, c, pt: (b, 0, 0)),
            pl.BlockSpec(memory_space=pl.ANY),
            pl.BlockSpec(memory_space=pl.ANY),
        ],
        out_specs=pl.BlockSpec((1, nh, MLA_KV_LORA), lambda b, c, pt: (b, 0, 0)),
        scratch_shapes=[
            pltpu.VMEM((MLA_DECODE_SLOTS, pages * page, MLA_KV_LORA), F32),
            pltpu.VMEM((MLA_DECODE_SLOTS, MLA_ROPE, pages * page), F32),
            pltpu.SemaphoreType.DMA((2, MLA_DECODE_SLOTS)),
            pltpu.VMEM((ns, nh, 1), F32), pltpu.VMEM((ns, nh, 1), F32), pltpu.VMEM((ns, nh, MLA_KV_LORA), F32),
        ],
    )
    return pl.pallas_call(
        functools.partial(_mla_decode_kernel, layer=layer, pages=pages),
        grid_spec=grid_spec,
        out_shape=jax.ShapeDtypeStruct((bsz, nh, MLA_KV_LORA), BF16),
        compiler_params=_params("arbitrary", "arbitrary"),
        name="mla_decode",
    )(page_table, q_lat, q_rope, ckv_new, kr_new, lat_cache, rope_cache_t)


PROMPT_TM = 512
FFN_TK = 1024
OUT_TK = 2048
SWA_QKV_TN = 2560
GDN_IN_TM = 1024
GDN_IN_TN = 2048
GDN_ROWS = 256
MLA_TQ = 1024
MLA_PAGES_PER_STEP = 32
GDN_CONV_TR = 512
GDN_CONV_TC = 1024


def _swa_weights(w_qkv, b_qkv, n_heads):
    q_dim = n_heads * SWA_HEAD_DIM
    col_scale = jnp.where(jnp.arange(w_qkv.shape[1]) < q_dim, SWA_HEAD_DIM ** -0.5, 1.0).astype(F32)
    return (w_qkv * col_scale).astype(BF16), b_qkv * col_scale


def _swa_layer(a_p, a_s, rope_p, rope_s, w_qkv, b_qkv, sinks, ck, cv, batch, seq):
    n_heads = sinks.shape[0]
    kvh = ck.shape[2]
    q_dim, kv_dim = n_heads * SWA_HEAD_DIM, kvh * SWA_HEAD_DIM
    w, b = _swa_weights(w_qkv, b_qkv, n_heads)
    rope_cols = (0, q_dim + kv_dim)
    qkv_p = _matmul(a_p, w, PROMPT_TM, SWA_QKV_TN, bias=b, rope=rope_p, rope_cols=rope_cols, name="swa_qkv")
    o_p = _swa_prompt_attn(qkv_p, sinks, batch, seq)
    rows = min(SWA_WINDOW, seq)
    kv_tail = qkv_p.reshape(batch, seq, -1)[:, seq - rows:, q_dim:].astype(F32)
    k_p = kv_tail[..., :kv_dim].reshape(batch, rows, kvh, SWA_HEAD_DIM)
    v_p = kv_tail[..., kv_dim:].reshape(batch, rows, kvh, SWA_HEAD_DIM)
    qkv_s = _matmul(a_s, w, a_s.shape[0], 512, bias=b, rope=rope_s, rope_cols=rope_cols, out_dtype=F32,
                    name="swa_qkv")
    o_s, k_s, v_s = _swa_sample_attn(qkv_s, ck, cv, sinks)
    return o_p, o_s, (k_p, v_p, k_s, v_s)


def _gdn_layer(a_p, a_s, w_in, conv_w, a_log, dt_bias, norm_w, ssm, conv_state, batch, seq):
    n_v = a_log.shape[0]
    n_qk = n_v // GDN_V_PER_QK
    key_dim = n_qk * GDN_HEAD_DIM
    conv_dim = 2 * key_dim + n_v * GDN_HEAD_DIM
    main_cols = conv_dim + n_v * GDN_HEAD_DIM
    w_main = w_in[:, :main_cols].astype(BF16)
    w_ba = jnp.pad(w_in[:, main_cols:], ((0, 0), (0, LANES - 2 * n_v))).astype(BF16)
    proj_p = _matmul(a_p, w_main, GDN_IN_TM, GDN_IN_TN, name="gdn_in")
    ba_p = _matmul(a_p, w_ba, PROMPT_TM, LANES, out_dtype=F32, name="gdn_in_gates")
    beta_p, _, gc_p = _gdn_gates(ba_p, a_log, dt_bias, GDN_ROWS, GDN_CHUNK)
    qkv_p = _gdn_conv(proj_p, conv_w, seq, key_dim, conv_dim, GDN_CONV_TR, GDN_CONV_TC)
    y_p, s_p = _gdn_scan(qkv_p, proj_p, gc_p, beta_p, gc_p[:, :n_v].T, norm_w, batch, seq, n_qk, GDN_ROWS, GDN_CHUNK)
    taps = GDN_CONV - 1
    c_p = proj_p.reshape(batch, seq, -1)[:, seq - taps:, :conv_dim].astype(F32)
    bsz = a_s.shape[0]
    proj_s = _matmul(a_s, w_main, bsz, 1024, out_dtype=F32, name="gdn_in")
    ba_s = _matmul(a_s, w_ba, bsz, LANES, out_dtype=F32, name="gdn_in_gates")
    beta_s, g_s, _ = _gdn_gates(ba_s, a_log, dt_bias, bsz, 1)
    qkv_s = _gdn_conv_step(conv_state, proj_s, conv_w, key_dim, 1024)
    y_s, s_s = _gdn_step(qkv_s, proj_s, g_s, beta_s, norm_w, ssm, n_qk)
    c_s = jnp.concatenate([conv_state[:, 1:], proj_s[:, None, :conv_dim]], axis=1)
    return y_p, y_s, (s_p, c_p, s_s, c_s)


def _mla_weights(w_in, w_uq):
    zeros = jnp.zeros((w_in.shape[0], MLA_ROPE), w_in.dtype)
    kr = w_in[:, MLA_Q_LORA + MLA_KV_LORA:]
    w_in_p = jnp.concatenate([w_in[:, :MLA_Q_LORA + MLA_KV_LORA], kr, zeros, zeros, kr], axis=1)
    wq = w_uq.reshape(MLA_Q_LORA, MLA_HEADS, MLA_NOPE + MLA_ROPE) * ((MLA_NOPE + MLA_ROPE) ** -0.5 * LOG2E)
    wq = jnp.concatenate([wq[..., :MLA_NOPE].reshape(MLA_Q_LORA, -1), wq[..., MLA_NOPE:].reshape(MLA_Q_LORA, -1)],
                         axis=1)
    return w_in_p.astype(BF16), wq.astype(BF16)


def _mla_layer(a_p, a_s, rope_p, rope_s, w_in, q_norm, w_uq, kv_norm, w_ukv, lat_cache, rope_cache, page_table,
               layer, batch, seq):
    w_in_p, wq = _mla_weights(w_in, w_uq)
    w_ukv = w_ukv.astype(BF16)
    nope = MLA_HEADS * MLA_NOPE
    q_p, kv_p, ckv_p, kr_p = _mla_project(a_p, w_in_p, q_norm, wq, kv_norm, rope_p, 256, w_ukv=w_ukv)
    o_p = _mla_prompt_attn(q_p, kv_p, batch, seq, min(MLA_TQ, seq))
    bsz = a_s.shape[0]
    q_s, ckv_s, kr_s = _mla_project(a_s, w_in_p, q_norm, wq, kv_norm, rope_s, bsz)
    q_lat = _head_matmul(q_s[:, :nope], w_ukv, True).reshape(bsz, MLA_HEADS, MLA_KV_LORA)
    q_rope = q_s[:, nope:].reshape(bsz, MLA_HEADS, MLA_ROPE)
    kr_new = kr_s[:, :MLA_ROPE]
    o_lat = _mla_decode(q_lat, q_rope, ckv_s.reshape(bsz, 1, -1), kr_new.reshape(bsz, 1, -1), lat_cache, rope_cache,
                        page_table, layer, MLA_PAGES_PER_STEP)
    o_s = _head_matmul(o_lat.reshape(bsz, -1), w_ukv, False)
    outs = (ckv_p.reshape(batch, seq, -1), kr_p[:, :MLA_ROPE].reshape(batch, seq, -1),
            ckv_s.reshape(bsz, 1, -1), kr_new.reshape(bsz, 1, -1))
    return o_p, o_s, outs


def kernel(x_prompt, x_sample, cache_swa_k, cache_swa_v, state_gdn_ssm, state_gdn_conv, cache_mla_latent, cache_mla_krope, page_table, norm_mix_pre, norm_mix_post, norm_ffn_pre, norm_ffn_post, ffn_w_up, ffn_w_down, swa_w_qkv, swa_b_qkv, swa_w_o, swa_sinks, gdn_w_in, gdn_conv_w, gdn_A_log, gdn_dt_bias, gdn_norm, gdn_w_out, mla_w_in, mla_q_norm, mla_w_uq, mla_kv_norm, mla_w_ukv, mla_w_o):
    batch, seq, d_model = x_prompt.shape
    bsz, dec_seq, _ = x_sample.shape
    assert dec_seq == 1
    depth = norm_mix_pre.shape[0]
    past_len = page_table.shape[1] * cache_mla_latent.shape[2]
    hp = x_prompt.reshape(batch * seq, d_model)
    hs = x_sample.reshape(bsz, d_model)
    rope_p = _rope_tables(jnp.arange(seq, dtype=jnp.int32))
    rope_s = _rope_tables(jnp.full((bsz,), past_len, jnp.int32))
    a_p = _rmsnorm(hp, norm_mix_pre[0], PROMPT_TM)
    a_s = _rmsnorm(hs, norm_mix_pre[0], bsz)
    collected = ([], [], [])
    w_up, w_down = ffn_w_up.astype(BF16), ffn_w_down.astype(BF16)
    w_outs = (swa_w_o.astype(BF16), gdn_w_out.astype(BF16), mla_w_o.astype(BF16))
    for layer in range(depth):
        kind, j = layer % 3, layer // 3
        if kind == 0:
            x_p, x_s, outs = _swa_layer(a_p, a_s, rope_p, rope_s, swa_w_qkv[j], swa_b_qkv[j], swa_sinks[j],
                                        cache_swa_k[j], cache_swa_v[j], batch, seq)
        elif kind == 1:
            x_p, x_s, outs = _gdn_layer(a_p, a_s, gdn_w_in[j], gdn_conv_w[j], gdn_A_log[j], gdn_dt_bias[j],
                                        gdn_norm[j], state_gdn_ssm[j], state_gdn_conv[j], batch, seq)
        else:
            x_p, x_s, outs = _mla_layer(a_p, a_s, rope_p, rope_s, mla_w_in[j], mla_q_norm[j], mla_w_uq[j],
                                        mla_kv_norm[j], mla_w_ukv[j], cache_mla_latent, cache_mla_krope,
                                        page_table, j, batch, seq)
        collected[kind].append(outs)
        w_o = w_outs[kind]
        g_next = norm_mix_pre[layer + 1] if layer + 1 < depth else None
        hp, a_p = _proj_res(x_p, w_o, hp, norm_mix_post[layer], norm_ffn_pre[layer], PROMPT_TM, OUT_TK,
                            layer=j, name="mixer_out")
        hs, a_s = _proj_res(x_s, w_o, hs, norm_mix_post[layer], norm_ffn_pre[layer], bsz, OUT_TK,
                            layer=j, name="mixer_out")
        hp, a_p = _proj_res(a_p, w_down, hp, norm_ffn_post[layer], g_next, PROMPT_TM, FFN_TK, w_up=w_up,
                            layer=layer, name="ffn")
        hs, a_s = _proj_res(a_s, w_down, hs, norm_ffn_post[layer], g_next, bsz, FFN_TK, w_up=w_up,
                            layer=layer, name="ffn")

    def stacked(kind):
        return tuple(jnp.stack(parts) for parts in zip(*collected[kind]))

    return ((hp.reshape(batch, seq, d_model), hs.reshape(bsz, 1, d_model))
            + stacked(0) + stacked(1) + stacked(2))
```

```python
import functools
import math

import jax
import jax.numpy as jnp
from jax import lax
from jax.experimental import pallas as pl
from jax.experimental.pallas import tpu as pltpu

F32 = jnp.float32
BF16 = jnp.bfloat16

RMS_EPS = 1e-6
ROPE_THETA = 10000.0
NEG_BIG = -1e30

LANES = 128
VMEM_LIMIT_BYTES = 56 * 1024 * 1024

SWA_HEAD_DIM = 64
SWA_GROUP = 8
SWA_WINDOW = 128
GDN_HEAD_DIM = 128
GDN_V_PER_QK = 2
GDN_CONV = 4
GDN_CHUNK = 64
MLA_HEADS = 16
MLA_Q_LORA = 512
MLA_KV_LORA = 512
MLA_NOPE = 128
MLA_ROPE = 64
MLA_V = 128


def _params(*sem):
    return pltpu.CompilerParams(dimension_semantics=sem, vmem_limit_bytes=VMEM_LIMIT_BYTES)


def _rms(y, g):
    return y * lax.rsqrt(jnp.mean(y * y, axis=-1, keepdims=True) + RMS_EPS) * g


def _dot(a, b):
    return jnp.dot(a, b, preferred_element_type=F32)


def _dot_nt(a, b):
    return lax.dot_general(a, b, (((1,), (1,)), ((), ())), preferred_element_type=F32)


def _dot_tn(a, b):
    return lax.dot_general(a, b, (((0,), (0,)), ((), ())), preferred_element_type=F32)


def _bmm(a, b):
    return jnp.einsum("nmk,nkp->nmp", a, b, preferred_element_type=F32)


def _bmm_nt(a, b):
    return jnp.einsum("nmk,npk->nmp", a, b, preferred_element_type=F32)


def _sigmoid(x):
    return 1.0 / (1.0 + jnp.exp(-x))


def _silu(x):
    return x * _sigmoid(x)


def _softplus(x):
    return jnp.maximum(x, 0.0) + jnp.log(1.0 + jnp.exp(-jnp.abs(x)))


def _rope_block(y, cos, sin_signed):
    lane = lax.broadcasted_iota(jnp.int32, (1, LANES), 1)
    first = (lane % 64) < 32
    swapped = jnp.where(first, pltpu.roll(y, 96, 1), pltpu.roll(y, 32, 1))
    return y * cos + swapped * sin_signed


def _rope_tables(pos):
    half = SWA_HEAD_DIM // 2
    inv_freq = ROPE_THETA ** (-jnp.arange(half, dtype=F32) * (2.0 / SWA_HEAD_DIM))
    ang = pos.astype(F32)[:, None] * inv_freq[None, :]
    cos, sin = jnp.cos(ang), jnp.sin(ang)
    return jnp.tile(cos, (1, 4)), jnp.tile(jnp.concatenate([-sin, sin], axis=1), (1, 2))


def _rmsnorm_kernel(x_ref, g_ref, o_ref):
    o_ref[...] = _rms(x_ref[...], g_ref[...]).astype(o_ref.dtype)


def _rmsnorm(x, g, tm):
    m, d = x.shape
    return pl.pallas_call(
        _rmsnorm_kernel,
        grid=(m // tm,),
        in_specs=[pl.BlockSpec((tm, d), lambda i: (i, 0)), pl.BlockSpec((1, d), lambda i: (0, 0))],
        out_specs=pl.BlockSpec((tm, d), lambda i: (i, 0)),
        out_shape=jax.ShapeDtypeStruct((m, d), BF16),
        compiler_params=_params("parallel"),
        name="rmsnorm",
    )(x, g.reshape(1, d))


def _mm_kernel(*refs, has_bias, rope_cols, tn):
    a_ref, w_ref = refs[0], refs[1]
    idx = 2
    if has_bias:
        b_ref = refs[idx]
        idx += 1
    if rope_cols is not None:
        cos_ref, sin_ref = refs[idx], refs[idx + 1]
        idx += 2
    o_ref = refs[idx]
    y = _dot(a_ref[...], w_ref[...])
    if has_bias:
        y = y + b_ref[...]
    if rope_cols is None:
        o_ref[...] = y.astype(o_ref.dtype)
        return
    j = pl.program_id(1)
    cos, sin = cos_ref[...], sin_ref[...]
    for c in range(tn // LANES):
        col0 = j * tn + c * LANES
        on = (col0 >= rope_cols[0]) & (col0 < rope_cols[1])
        yc = y[:, c * LANES:(c + 1) * LANES]
        o_ref[:, c * LANES:(c + 1) * LANES] = jnp.where(on, _rope_block(yc, cos, sin), yc).astype(o_ref.dtype)


def _matmul(a, w, tm, tn, bias=None, rope=None, rope_cols=None, out_dtype=BF16, n_cols=None, name="matmul"):
    m, k = a.shape
    n = w.shape[1] if n_cols is None else n_cols
    assert m % tm == 0 and n % tn == 0 and n <= w.shape[1], (m, tm, n, tn)
    in_specs = [pl.BlockSpec((tm, k), lambda i, j: (i, 0)), pl.BlockSpec((k, tn), lambda i, j: (0, j))]
    args = [a, w]
    if bias is not None:
        in_specs.append(pl.BlockSpec((1, tn), lambda i, j: (0, j)))
        args.append(bias.reshape(1, n).astype(F32))
    if rope is not None:
        rope_blocks = rope[0].shape[0] // tm
        in_specs += [pl.BlockSpec((tm, LANES), lambda i, j: (i % rope_blocks, 0))] * 2
        args += list(rope)
    return pl.pallas_call(
        functools.partial(_mm_kernel, has_bias=bias is not None, rope_cols=rope_cols, tn=tn),
        grid=(m // tm, n // tn),
        in_specs=in_specs,
        out_specs=pl.BlockSpec((tm, tn), lambda i, j: (i, j)),
        out_shape=jax.ShapeDtypeStruct((m, n), out_dtype),
        compiler_params=_params("parallel", "arbitrary"),
        name=name,
    )(*args)


PROJ_PIECE = 512


def _proj_res_kernel(*refs, has_up, has_next, nk):
    if has_up:
        x_ref, wu_ref, w_ref, h_ref, gp_ref = refs[:5]
        idx = 5
    else:
        x_ref, w_ref, h_ref, gp_ref = refs[:4]
        idx = 4
    if has_next:
        gn_ref = refs[idx]
        idx += 1
    ho_ref = refs[idx]
    idx += 1
    if has_next:
        ao_ref = refs[idx]
        idx += 1
    acc_ref = refs[idx] if nk > 1 else None

    def finish(y):
        hn = h_ref[...] + _rms(y, gp_ref[...])
        ho_ref[...] = hn
        if has_next:
            ao_ref[...] = _rms(hn, gn_ref[...]).astype(BF16)

    if has_up:
        tk = wu_ref.shape[1]
        piece = min(tk, PROJ_PIECE)
        ups = []
        for p in range(tk // piece):
            u = jnp.maximum(_dot(x_ref[...], wu_ref[:, p * piece:(p + 1) * piece]), 0.0)
            ups.append((u * u).astype(BF16))
        x = jnp.concatenate(ups, axis=1) if len(ups) > 1 else ups[0]
    else:
        x = x_ref[...]

    if nk == 1:
        finish(_dot(x, w_ref[...]))
        return
    k = pl.program_id(1)

    @pl.when(k == 0)
    def _():
        acc_ref[...] = jnp.zeros_like(acc_ref)

    d = w_ref.shape[1]
    piece = min(d, PROJ_PIECE)
    for n in range(d // piece):
        cols = slice(n * piece, (n + 1) * piece)
        acc_ref[:, cols] += _dot(x, w_ref[:, cols])

    @pl.when(k == nk - 1)
    def _():
        finish(acc_ref[...])


def _proj_res(x, w, h, g_post, g_next, tm, tk, w_up=None, layer=0, name="proj_res"):
    m, d = h.shape
    kdim = w.shape[1]
    nk = kdim // tk
    has_up, has_next = w_up is not None, g_next is not None
    if has_up:
        dk = x.shape[1]
        in_specs = [pl.BlockSpec((tm, dk), lambda i, k: (i, 0)),
                    pl.BlockSpec((None, dk, tk), lambda i, k: (layer, 0, k)),
                    pl.BlockSpec((None, tk, d), lambda i, k: (layer, k, 0))]
        args = [x, w_up, w]
    else:
        in_specs = [pl.BlockSpec((tm, tk), lambda i, k: (i, k)),
                    pl.BlockSpec((None, tk, d), lambda i, k: (layer, k, 0))]
        args = [x, w]
    row = pl.BlockSpec((tm, d), lambda i, k: (i, 0))
    vec = pl.BlockSpec((1, d), lambda i, k: (0, 0))
    in_specs += [row, vec]
    args += [h, g_post.reshape(1, d)]
    out_specs, out_shape = [row], [jax.ShapeDtypeStruct((m, d), F32)]
    if has_next:
        in_specs.append(vec)
        args.append(g_next.reshape(1, d))
        out_specs.append(row)
        out_shape.append(jax.ShapeDtypeStruct((m, d), BF16))
    outs = pl.pallas_call(
        functools.partial(_proj_res_kernel, has_up=has_up, has_next=has_next, nk=nk),
        grid=(m // tm, nk),
        in_specs=in_specs,
        out_specs=out_specs,
        out_shape=out_shape,
        scratch_shapes=[pltpu.VMEM((tm, d), F32)] if nk > 1 else [],
        compiler_params=_params("parallel", "arbitrary"),
        name=name,
    )(*args)
    return (outs[0], outs[1]) if has_next else (outs[0], None)


def _softmax_sink_unnorm(s, sink):
    m = jnp.maximum(jnp.max(s, axis=-1, keepdims=True), sink)
    p = jnp.exp(s - m)
    return p, 1.0 / (jnp.sum(p, axis=-1, keepdims=True) + jnp.exp(sink - m))


def _swa_prompt_kernel(sink_ref, q_ref, kp_ref, kc_ref, vp_ref, vc_ref, o_ref, *, nb, n_kvh):
    w = SWA_WINDOW
    n = pl.program_id(0) % nb
    lane = lax.broadcasted_iota(jnp.int32, (1, LANES), 1)
    pairs = SWA_GROUP // 2
    pair_w = pairs * LANES

    def pair_operand(p_ref, c_ref, kvh):
        blk0 = (kvh // 2) * LANES
        blk = jnp.concatenate([p_ref[:, blk0:blk0 + LANES], c_ref[:, blk0:blk0 + LANES]], axis=0).astype(F32)
        odd = kvh % 2 == 1
        own = jnp.where((lane >= 64) == odd, blk, 0.0)
        other = pltpu.roll(own, 64, 1)
        lo, hi = (other, own) if odd else (own, other)
        return jnp.concatenate([lo, hi], axis=0).astype(BF16)

    qi = lax.broadcasted_iota(jnp.int32, (w, 2 * w), 0)
    kj = lax.broadcasted_iota(jnp.int32, (w, 2 * w), 1)
    rel = qi + w - kj
    mask = (rel >= 0) & (rel < w) & ((n > 0) | (kj >= w))
    k2 = [pair_operand(kp_ref, kc_ref, kvh) for kvh in range(n_kvh)]
    v2 = [pair_operand(vp_ref, vc_ref, kvh) for kvh in range(n_kvh)]
    s2 = [[_dot_nt(q_ref[:, kvh * pair_w + p * LANES:kvh * pair_w + (p + 1) * LANES], k2[kvh])
           for p in range(pairs)] for kvh in range(n_kvh)]
    probs, scales = [], []
    for kvh in range(n_kvh):
        for p in range(pairs):
            ps, rs = [], []
            for e in range(2):
                s = jnp.where(mask, s2[kvh][p][:, e * 2 * w:(e + 1) * 2 * w], NEG_BIG)
                pe, re = _softmax_sink_unnorm(s, sink_ref[kvh * SWA_GROUP + 2 * p + e])
                ps.append(pe.astype(BF16))
                rs.append(re)
            probs.append(jnp.concatenate(ps, axis=1))
            scales.append(jnp.where(lane < 64, rs[0], rs[1]))
    for kvh in range(n_kvh):
        for p in range(pairs):
            i = kvh * pairs + p
            o_ref[:, i * LANES:(i + 1) * LANES] = (_dot(probs[i], v2[kvh]) * scales[i]).astype(o_ref.dtype)


def _swa_prompt_attn(qkv, sinks, batch, seq):
    w = SWA_WINDOW
    nb = seq // w
    n_heads = sinks.shape[0]
    kvh = n_heads // SWA_GROUP
    q_dim, kv_dim = n_heads * SWA_HEAD_DIM, kvh * SWA_HEAD_DIM
    kblk0 = q_dim // kv_dim
    vblk0 = kblk0 + 1

    def prev(i):
        return jnp.where(i % nb == 0, i, i - 1)

    return pl.pallas_call(
        functools.partial(_swa_prompt_kernel, nb=nb, n_kvh=kvh),
        grid=(batch * nb,),
        in_specs=[
            pl.BlockSpec(memory_space=pltpu.SMEM),
            pl.BlockSpec((w, q_dim), lambda i: (i, 0)),
            pl.BlockSpec((w, kv_dim), lambda i: (prev(i), kblk0)),
            pl.BlockSpec((w, kv_dim), lambda i: (i, kblk0)),
            pl.BlockSpec((w, kv_dim), lambda i: (prev(i), vblk0)),
            pl.BlockSpec((w, kv_dim), lambda i: (i, vblk0)),
        ],
        out_specs=pl.BlockSpec((w, q_dim), lambda i: (i, 0)),
        out_shape=jax.ShapeDtypeStruct((batch * seq, q_dim), BF16),
        compiler_params=_params("parallel"),
        name="swa_prompt_attn",
    )(sinks.astype(F32), qkv, qkv, qkv, qkv, qkv)


def _swa_sample_kernel(q_ref, kn_ref, vn_ref, ck_ref, cv_ref, sink_ref, o_ref, cko_ref, cvo_ref, *, bt, npair):
    p_rows = ck_ref.shape[1]
    grows = 2 * SWA_GROUP
    lane = lax.broadcasted_iota(jnp.int32, (1, 1, LANES), 2)
    row = lax.broadcasted_iota(jnp.int32, (1, p_rows, 1), 1)
    grp = lax.broadcasted_iota(jnp.int32, (1, grows, 1), 1)
    k_new, v_new = kn_ref[...], vn_ref[...]
    cko_ref[:, pl.ds(0, p_rows - 1), :] = ck_ref[:, pl.ds(1, p_rows - 1), :]
    cvo_ref[:, pl.ds(0, p_rows - 1), :] = cv_ref[:, pl.ds(1, p_rows - 1), :]
    cko_ref[:, pl.ds(p_rows - 1, 1), :] = k_new
    cvo_ref[:, pl.ds(p_rows - 1, 1), :] = v_new
    qs, kks, vvs, sinks = [], [], [], []
    for c in range(npair):
        sl = slice(c * LANES, (c + 1) * LANES)
        kks.append(jnp.where(row == 0, k_new[:, :, sl], ck_ref[:, :, sl]).astype(BF16))
        vvs.append(jnp.where(row == 0, v_new[:, :, sl], cv_ref[:, :, sl]).astype(BF16))
        q2 = q_ref[:, pl.ds(c * grows, grows), :]
        qs.append(jnp.where((lane >= 64) == (grp >= SWA_GROUP), q2, 0.0).astype(BF16))
        sinks.append(jnp.broadcast_to(sink_ref[c], (bt, grows, 1)))
    vv = jnp.concatenate(vvs, axis=0)
    s = _bmm_nt(jnp.concatenate(qs, axis=0), jnp.concatenate(kks, axis=0))
    pe, re = _softmax_sink_unnorm(s, jnp.concatenate(sinks, axis=0))
    o2 = _bmm(pe.astype(BF16), vv) * re
    for c in range(npair):
        oc = o2[c * bt:(c + 1) * bt]
        o_ref[:, pl.ds(c * SWA_GROUP, SWA_GROUP), :] = jnp.where(
            lane < 64, oc[:, :SWA_GROUP, :], oc[:, SWA_GROUP:, :]).astype(o_ref.dtype)


SWA_SAMPLE_BT = 16


def _swa_sample_attn(qkv, ck, cv, sinks):
    bsz, p_rows, kvh, hd = ck.shape
    assert p_rows == SWA_WINDOW and hd == SWA_HEAD_DIM
    n_heads = kvh * SWA_GROUP
    q_dim, kv_dim = n_heads * hd, kvh * hd
    npair = kvh // 2
    bt = SWA_SAMPLE_BT
    q = qkv[:, :q_dim].reshape(bsz, npair, 2, SWA_GROUP, hd).transpose(0, 1, 3, 2, 4)
    q = jnp.broadcast_to(q.reshape(bsz, npair, 1, SWA_GROUP, 2 * hd), (bsz, npair, 2, SWA_GROUP, 2 * hd))
    q = q.reshape(bsz, npair * 2 * SWA_GROUP, 2 * hd)
    k_new = qkv[:, q_dim:q_dim + kv_dim].reshape(bsz, 1, kv_dim)
    v_new = qkv[:, q_dim + kv_dim:].reshape(bsz, 1, kv_dim)
    sink = sinks.astype(F32).reshape(npair, 2 * SWA_GROUP, 1)
    o, cko, cvo = pl.pallas_call(
        functools.partial(_swa_sample_kernel, bt=bt, npair=npair),
        grid=(bsz // bt,),
        in_specs=[
            pl.BlockSpec((bt, npair * 2 * SWA_GROUP, LANES), lambda i: (i, 0, 0)),
            pl.BlockSpec((bt, 1, kv_dim), lambda i: (i, 0, 0)),
            pl.BlockSpec((bt, 1, kv_dim), lambda i: (i, 0, 0)),
            pl.BlockSpec((bt, p_rows, kv_dim), lambda i: (i, 0, 0)),
            pl.BlockSpec((bt, p_rows, kv_dim), lambda i: (i, 0, 0)),
            pl.BlockSpec((npair, 2 * SWA_GROUP, 1), lambda i: (0, 0, 0)),
        ],
        out_specs=[
            pl.BlockSpec((bt, npair * SWA_GROUP, LANES), lambda i: (i, 0, 0)),
            pl.BlockSpec((bt, p_rows, kv_dim), lambda i: (i, 0, 0)),
            pl.BlockSpec((bt, p_rows, kv_dim), lambda i: (i, 0, 0)),
        ],
        out_shape=[
            jax.ShapeDtypeStruct((bsz, npair * SWA_GROUP, LANES), BF16),
            jax.ShapeDtypeStruct((bsz, p_rows, kv_dim), F32),
            jax.ShapeDtypeStruct((bsz, p_rows, kv_dim), F32),
        ],
        compiler_params=_params("parallel"),
        name="swa_sample_attn",
    )(q, k_new, v_new, ck.reshape(bsz, p_rows, kv_dim), cv.reshape(bsz, p_rows, kv_dim), sink)
    o = o.reshape(bsz, npair, SWA_GROUP, 2, hd).transpose(0, 1, 3, 2, 4).reshape(bsz, q_dim)
    return o, cko.reshape(bsz, p_rows, kvh, hd), cvo.reshape(bsz, p_rows, kvh, hd)


def _gdn_gates_kernel(ba_ref, alog_ref, dt_ref, beta_ref, g_ref, gc_ref, *, chunk, nh):
    rows = ba_ref.shape[0]
    x = ba_ref[...]
    beta_ref[...] = _sigmoid(x)
    a = pltpu.roll(x, LANES - nh, 1)
    g = -jnp.exp(alog_ref[...]) * _softplus(a + dt_ref[...])
    g_ref[...] = g
    pos = lax.broadcasted_iota(jnp.int32, (rows, 1), 0) % chunk
    acc = g
    shift = 1
    while shift < chunk:
        acc = acc + jnp.where(pos >= shift, pltpu.roll(acc, shift, 0), 0.0)
        shift *= 2
    gc_ref[...] = acc


def _gdn_gates(ba, a_log, dt_bias, tr, chunk):
    m = ba.shape[0]
    nh = a_log.shape[0]
    row = pl.BlockSpec((tr, LANES), lambda i: (i, 0))
    vec = pl.BlockSpec((1, LANES), lambda i: (0, 0))

    def pad(x):
        return jnp.pad(x.astype(F32), (0, LANES - nh)).reshape(1, LANES)

    return pl.pallas_call(
        functools.partial(_gdn_gates_kernel, chunk=chunk, nh=nh),
        grid=(m // tr,),
        in_specs=[row, vec, vec],
        out_specs=[row, row, row],
        out_shape=[jax.ShapeDtypeStruct((m, LANES), F32)] * 3,
        compiler_params=_params("parallel"),
        name="gdn_gates",
    )(ba, pad(a_log), pad(dt_bias))


def _gdn_conv_kernel(x_ref, halo_ref, w_ref, o_ref, ext_ref, *, seq, key_dim, scale):
    tr, tc = x_ref.shape
    i, j = pl.program_id(0), pl.program_id(1)
    hb = halo_ref.shape[0]
    starts_seq = (i * tr) % seq == 0
    ext_ref[pl.ds(0, 8), :] = jnp.where(starts_seq, 0.0, halo_ref[pl.ds(hb - 8, 8), :].astype(F32))
    ext_ref[pl.ds(8, tr), :] = x_ref[...].astype(F32)
    acc = ext_ref[pl.ds(8 - (GDN_CONV - 1), tr), :] * w_ref[0:1, :]
    for t in range(1, GDN_CONV):
        acc = acc + ext_ref[pl.ds(8 - (GDN_CONV - 1) + t, tr), :] * w_ref[t:t + 1, :]
    c = _silu(acc)
    col0 = j * tc
    is_q = col0 < key_dim
    is_qk = col0 < 2 * key_dim
    for hblk in range(tc // LANES):
        ch = c[:, hblk * LANES:(hblk + 1) * LANES]
        nrm = ch * lax.rsqrt(jnp.sum(ch * ch, axis=-1, keepdims=True) + RMS_EPS)
        nrm = jnp.where(is_q, nrm * scale, nrm)
        o_ref[:, hblk * LANES:(hblk + 1) * LANES] = jnp.where(is_qk, nrm, ch).astype(o_ref.dtype)


def _gdn_conv(proj, conv_w, seq, key_dim, conv_dim, tr, tc):
    m = proj.shape[0]
    hb = 16
    return pl.pallas_call(
        functools.partial(_gdn_conv_kernel, seq=seq, key_dim=key_dim, scale=GDN_HEAD_DIM ** -0.5),
        grid=(m // tr, conv_dim // tc),
        in_specs=[
            pl.BlockSpec((tr, tc), lambda i, j: (i, j)),
            pl.BlockSpec((hb, tc), lambda i, j: (jnp.maximum(i * (tr // hb) - 1, 0), j)),
            pl.BlockSpec((GDN_CONV, tc), lambda i, j: (0, j)),
        ],
        out_specs=pl.BlockSpec((tr, tc), lambda i, j: (i, j)),
        out_shape=jax.ShapeDtypeStruct((m, conv_dim), BF16),
        scratch_shapes=[pltpu.VMEM((tr + 8, tc), F32)],
        compiler_params=_params("parallel", "arbitrary"),
        name="gdn_conv",
    )(proj, proj, conv_w.astype(F32))


GDN_INV_BASE = 4
GDN_QK_PER_STEP = 8


def _unit_lower_inverse(a):
    n, c, _ = a.shape
    base = GDN_INV_BASE
    ri = lax.broadcasted_iota(jnp.int32, (1, c, c), 1)
    ci = lax.broadcasted_iota(jnp.int32, (1, c, c), 2)
    x = jnp.broadcast_to((ri == ci).astype(F32), (n, c, c))
    diag_blocks = jnp.where(ri // base == ci // base, a, 0.0)
    col_in_block = ci % base
    for j in range(base - 1):
        coef = jnp.sum(jnp.where(col_in_block == j, diag_blocks, 0.0), axis=2, keepdims=True)
        rows = jnp.concatenate(
            [jnp.broadcast_to(x[:, b * base + j:b * base + j + 1, :], (n, base, c)) for b in range(c // base)], axis=1)
        x = x - coef * rows
    size = base
    while size < c:
        off = jnp.where((ri // (2 * size) == ci // (2 * size)) & (ri // size != ci // size), a, 0.0)
        xb = x.astype(BF16)
        x = x - _bmm(_bmm(xb, off.astype(BF16)).astype(BF16), xb)
        size *= 2
    return x


def _gdn_scan_kernel(q_ref, k_ref, v_ref, z_ref, gc_ref, beta_ref, gt_ref, nw_ref, y_ref, so_ref, s_ref, *, chunk):
    rows = q_ref.shape[0]
    hd = GDN_HEAD_DIM
    nc = rows // chunk
    n_qk, per = GDN_QK_PER_STEP, GDN_V_PER_QK
    n_v = n_qk * per
    hq0, t = pl.program_id(1) * n_qk, pl.program_id(2)

    @pl.when(t == 0)
    def _():
        s_ref[...] = jnp.zeros_like(s_ref)

    lane = lax.broadcasted_iota(jnp.int32, (1, LANES), 1)
    ri = lax.broadcasted_iota(jnp.int32, (1, chunk, chunk), 1)
    ci = lax.broadcasted_iota(jnp.int32, (1, chunk, chunk), 2)
    tri, strict = ri >= ci, ri > ci
    q3 = [q_ref[:, h * hd:(h + 1) * hd].reshape(nc, chunk, hd) for h in range(n_qk)]
    k3 = [k_ref[:, h * hd:(h + 1) * hd].reshape(nc, chunk, hd) for h in range(n_qk)]
    gkk = [_bmm_nt(k3[h], k3[h]) for h in range(n_qk)]
    gqk = [_bmm_nt(q3[h], k3[h]) for h in range(n_qk)]
    bcol, eg, w_last, eg_last, lower, qk = [], [], [], [], [], []
    for i in range(n_v):
        pick = lane == hq0 * per + i
        gcol = jnp.sum(jnp.where(pick, gc_ref[...], 0.0), axis=1, keepdims=True).reshape(nc, chunk, 1)
        bcol.append(jnp.sum(jnp.where(pick, beta_ref[...], 0.0), axis=1, keepdims=True).reshape(nc, chunk, 1))
        grow_full = gt_ref[pl.ds(hq0 * per + i, 1), :]
        grow = jnp.stack([grow_full[:, c * chunk:(c + 1) * chunk] for c in range(nc)], axis=0)
        decay = jnp.where(tri, jnp.exp(jnp.where(tri, gcol - grow, 0.0)), 0.0)
        lower.append(jnp.where(strict, bcol[i] * gkk[i // per] * decay, 0.0))
        qk.append((gqk[i // per] * decay).astype(BF16))
        g_last = gcol[:, chunk - 1:chunk, :]
        eg.append(jnp.exp(gcol))
        w_last.append(jnp.exp(g_last - gcol))
        eg_last.append(jnp.exp(g_last))
    tinv = _unit_lower_inverse(jnp.concatenate(lower, axis=0)).astype(BF16)

    state = [s_ref[h] for h in range(n_qk)]
    for c in range(nc):
        r0 = c * chunk
        sb = [state[h].astype(BF16) for h in range(n_qk)]
        ks = [_dot(k3[h][c], sb[h]) for h in range(n_qk)]
        qs = [_dot(q3[h][c], sb[h]) for h in range(n_qk)]
        rhs = []
        for i in range(n_v):
            h, e = divmod(i, per)
            v = v_ref[pl.ds(r0, chunk), pl.ds(i * hd, hd)].astype(F32)
            rhs.append((bcol[i][c] * (v - eg[i][c] * ks[h][:, e * hd:(e + 1) * hd])).astype(BF16))
        v_new = _bmm(jnp.stack([tinv[i * nc + c] for i in range(n_v)]), jnp.stack(rhs))
        o_intra = _bmm(jnp.stack([qk[i][c] for i in range(n_v)]), v_new.astype(BF16))
        for i in range(n_v):
            h, e = divmod(i, per)
            o = eg[i][c] * qs[h][:, e * hd:(e + 1) * hd] + o_intra[i]
            gate = _silu(z_ref[pl.ds(r0, chunk), pl.ds(i * hd, hd)].astype(F32))
            y_ref[pl.ds(r0, chunk), pl.ds(i * hd, hd)] = (_rms(o, nw_ref[...]) * gate).astype(y_ref.dtype)
        for h in range(n_qk):
            mine = range(h * per, (h + 1) * per)
            wv = jnp.concatenate([(w_last[i][c] * v_new[i]).astype(BF16) for i in mine], axis=1)
            keep = jnp.concatenate([jnp.broadcast_to(eg_last[i][c], (1, hd)) for i in mine], axis=1)
            state[h] = keep * state[h] + _dot_tn(k3[h][c], wv)
    for h in range(n_qk):
        s_ref[h] = state[h]

    @pl.when(t == pl.num_programs(2) - 1)
    def _():
        for i in range(n_v):
            h, e = divmod(i, per)
            so_ref[0, i] = state[h][:, e * hd:(e + 1) * hd]


def _gdn_scan(qkv, proj, gc, beta, gc_t, norm_w, batch, seq, n_qk, rows, chunk):
    hd = GDN_HEAD_DIM
    m = batch * seq
    nt = seq // rows
    n_v = n_qk * GDN_V_PER_QK
    qw = GDN_QK_PER_STEP * hd
    n_local = GDN_QK_PER_STEP * GDN_V_PER_QK
    vw = n_local * hd
    k_blk0 = n_qk * hd // qw
    v_blk0 = 2 * n_qk * hd // vw
    z_blk0 = (2 * n_qk * hd + n_v * hd) // vw
    y, s_fin = pl.pallas_call(
        functools.partial(_gdn_scan_kernel, chunk=chunk),
        grid=(batch, n_qk // GDN_QK_PER_STEP, nt),
        in_specs=[
            pl.BlockSpec((rows, qw), lambda b, h, t: (b * nt + t, h)),
            pl.BlockSpec((rows, qw), lambda b, h, t: (b * nt + t, k_blk0 + h)),
            pl.BlockSpec((rows, vw), lambda b, h, t: (b * nt + t, v_blk0 + h)),
            pl.BlockSpec((rows, vw), lambda b, h, t: (b * nt + t, z_blk0 + h)),
            pl.BlockSpec((rows, LANES), lambda b, h, t: (b * nt + t, 0)),
            pl.BlockSpec((rows, LANES), lambda b, h, t: (b * nt + t, 0)),
            pl.BlockSpec((n_v, rows), lambda b, h, t: (0, b * nt + t)),
            pl.BlockSpec((1, hd), lambda b, h, t: (0, 0)),
        ],
        out_specs=[
            pl.BlockSpec((rows, vw), lambda b, h, t: (b * nt + t, h)),
            pl.BlockSpec((1, n_local, hd, hd), lambda b, h, t: (b, h, 0, 0)),
        ],
        out_shape=[
            jax.ShapeDtypeStruct((m, n_v * hd), BF16),
            jax.ShapeDtypeStruct((batch, n_v, hd, hd), F32),
        ],
        scratch_shapes=[pltpu.VMEM((GDN_QK_PER_STEP, hd, GDN_V_PER_QK * hd), F32)],
        compiler_params=_params("parallel", "parallel", "arbitrary"),
        name="gdn_scan",
    )(qkv, qkv, qkv, proj, gc, beta, gc_t, norm_w.reshape(1, hd).astype(F32))
    return y, s_fin


def _gdn_conv_step_kernel(s0_ref, s1_ref, s2_ref, x_ref, w_ref, o_ref, *, key_dim, scale):
    tc = x_ref.shape[1]
    acc = s0_ref[...] * w_ref[0:1, :] + s1_ref[...] * w_ref[1:2, :] + s2_ref[...] * w_ref[2:3, :] \
        + x_ref[...] * w_ref[3:4, :]
    c = _silu(acc)
    col0 = pl.program_id(0) * tc
    is_q = col0 < key_dim
    is_qk = col0 < 2 * key_dim
    for hblk in range(tc // LANES):
        ch = c[:, hblk * LANES:(hblk + 1) * LANES]
        nrm = ch * lax.rsqrt(jnp.sum(ch * ch, axis=-1, keepdims=True) + RMS_EPS)
        nrm = jnp.where(is_q, nrm * scale, nrm)
        o_ref[:, hblk * LANES:(hblk + 1) * LANES] = jnp.where(is_qk, nrm, ch)


def _gdn_conv_step(conv_state, mixed, conv_w, key_dim, tc):
    bsz, taps, cdim = conv_state.shape
    assert taps == GDN_CONV - 1
    ncol = cdim // tc
    flat = conv_state.reshape(bsz, taps * cdim)
    return pl.pallas_call(
        functools.partial(_gdn_conv_step_kernel, key_dim=key_dim, scale=GDN_HEAD_DIM ** -0.5),
        grid=(ncol,),
        in_specs=[
            pl.BlockSpec((bsz, tc), lambda j: (0, j)),
            pl.BlockSpec((bsz, tc), lambda j: (0, ncol + j)),
            pl.BlockSpec((bsz, tc), lambda j: (0, 2 * ncol + j)),
            pl.BlockSpec((bsz, tc), lambda j: (0, j)),
            pl.BlockSpec((GDN_CONV, tc), lambda j: (0, j)),
        ],
        out_specs=pl.BlockSpec((bsz, tc), lambda j: (0, j)),
        out_shape=jax.ShapeDtypeStruct((bsz, cdim), F32),
        compiler_params=_params("parallel"),
        name="gdn_conv_step",
    )(flat, flat, flat, mixed, conv_w.astype(F32))


def _gdn_step_kernel(q_ref, k_ref, v_ref, z_ref, g_ref, beta_ref, nw_ref, s_ref, y_ref, so_ref, o_scr):
    bsz = q_ref.shape[0]
    hv = pl.program_id(0)
    pick = lax.broadcasted_iota(jnp.int32, (1, LANES), 1) == hv
    egcol = jnp.exp(jnp.sum(jnp.where(pick, g_ref[...], 0.0), axis=1, keepdims=True))
    bcol = jnp.sum(jnp.where(pick, beta_ref[...], 0.0), axis=1, keepdims=True)
    q_t = q_ref[...].T
    k_t = k_ref[...].T
    for b in range(bsz):
        kc, qc = k_t[:, b:b + 1], q_t[:, b:b + 1]
        s1 = s_ref[b, 0] * egcol[b:b + 1, :]
        delta = (v_ref[b:b + 1, :] - jnp.sum(s1 * kc, axis=0, keepdims=True)) * bcol[b:b + 1, :]
        s2 = s1 + kc * delta
        so_ref[b, 0] = s2
        o_scr[b:b + 1, :] = jnp.sum(s2 * qc, axis=0, keepdims=True)
    y_ref[...] = (_rms(o_scr[...], nw_ref[...]) * _silu(z_ref[...])).astype(y_ref.dtype)


def _gdn_step(qkv, proj, g, beta, norm_w, ssm, n_qk):
    bsz, n_v, hd, _ = ssm.shape
    k_blk0, v_blk0 = n_qk, 2 * n_qk
    z_blk0 = 2 * n_qk + n_v
    row = pl.BlockSpec((bsz, LANES), lambda h: (0, 0))
    st = pl.BlockSpec((bsz, 1, hd, hd), lambda h: (0, h, 0, 0))
    return pl.pallas_call(
        _gdn_step_kernel,
        grid=(n_v,),
        in_specs=[
            pl.BlockSpec((bsz, hd), lambda h: (0, h // GDN_V_PER_QK)),
            pl.BlockSpec((bsz, hd), lambda h: (0, k_blk0 + h // GDN_V_PER_QK)),
            pl.BlockSpec((bsz, hd), lambda h: (0, v_blk0 + h)),
            pl.BlockSpec((bsz, hd), lambda h: (0, z_blk0 + h)),
            row, row,
            pl.BlockSpec((1, hd), lambda h: (0, 0)),
            st,
        ],
        out_specs=[pl.BlockSpec((bsz, hd), lambda h: (0, h)), st],
        out_shape=[jax.ShapeDtypeStruct((bsz, n_v * hd), BF16), jax.ShapeDtypeStruct(ssm.shape, F32)],
        scratch_shapes=[pltpu.VMEM((bsz, hd), F32)],
        compiler_params=_params("parallel"),
        name="gdn_step",
    )(qkv, qkv, qkv, proj, g, beta, norm_w.reshape(1, hd).astype(F32), ssm)


MLA_IN_COLS = MLA_Q_LORA + MLA_KV_LORA + 2 * LANES
MLA_Q_COLS = MLA_HEADS * (MLA_NOPE + MLA_ROPE)
MLA_QK_W = MLA_NOPE + LANES
MLA_KV_W = MLA_QK_W + MLA_V
LOG2E = math.log2(math.e)


def _mla_proj_kernel(*refs, has_kv):
    a_ref, win_ref, qn_ref, kvn_ref, wuq_ref = refs[:5]
    idx = 5
    if has_kv:
        wukv_ref = refs[idx]
        idx += 1
    cos_ref, sin_ref, q_ref = refs[idx:idx + 3]
    idx += 3
    if has_kv:
        kv_ref = refs[idx]
        idx += 1
    ckv_ref, kr_ref = refs[idx:idx + 2]
    cos, sin = cos_ref[...], sin_ref[...]
    p = _dot(a_ref[...], win_ref[...])
    cq = _rms(p[:, :MLA_Q_LORA], qn_ref[...]).astype(BF16)
    ckv = _rms(p[:, MLA_Q_LORA:MLA_Q_LORA + MLA_KV_LORA], kvn_ref[...])
    ckv_ref[...] = ckv
    r0 = MLA_Q_LORA + MLA_KV_LORA
    kr_ref[...] = _rope_block(p[:, r0:r0 + LANES], cos, sin)
    q = _dot(cq, wuq_ref[...])
    nope = MLA_HEADS * MLA_NOPE
    q_rope = [_rope_block(q[:, nope + c * LANES:nope + (c + 1) * LANES], cos, sin).astype(q_ref.dtype)
              for c in range(MLA_HEADS * MLA_ROPE // LANES)]
    if not has_kv:
        q_ref[:, :nope] = q[:, :nope].astype(q_ref.dtype)
        for c, qr in enumerate(q_rope):
            q_ref[:, nope + c * LANES:nope + (c + 1) * LANES] = qr
        return
    k_rope = [kr_ref[...].astype(kv_ref.dtype),
              _rope_block(p[:, r0 + LANES:r0 + 2 * LANES], cos, sin).astype(kv_ref.dtype)]
    kv = _dot(ckv.astype(BF16), wukv_ref[...])
    hd = MLA_NOPE
    for h in range(MLA_HEADS):
        q_ref[:, MLA_QK_W * h:MLA_QK_W * h + hd] = q[:, hd * h:hd * (h + 1)].astype(q_ref.dtype)
        q_ref[:, MLA_QK_W * h + hd:MLA_QK_W * (h + 1)] = q_rope[h // 2]
        k0 = MLA_KV_W * h
        kv_ref[:, k0:k0 + hd] = kv[:, 2 * hd * h:2 * hd * h + hd].astype(kv_ref.dtype)
        kv_ref[:, k0 + hd:k0 + MLA_QK_W] = k_rope[h % 2]
        kv_ref[:, k0 + MLA_QK_W:k0 + MLA_KV_W] = kv[:, 2 * hd * h + hd:2 * hd * (h + 1)].astype(kv_ref.dtype)


def _mla_project(a, w_in, q_norm, w_uq, kv_norm, rope, tm, w_ukv=None):
    m, d = a.shape
    has_kv = w_ukv is not None
    row = lambda n: pl.BlockSpec((tm, n), lambda i: (i, 0))
    full = lambda s: pl.BlockSpec(s, lambda i: (0, 0))
    in_specs = [row(d), full(w_in.shape), full((1, MLA_Q_LORA)), full((1, MLA_KV_LORA)), full(w_uq.shape)]
    args = [a, w_in, q_norm.reshape(1, -1).astype(F32), kv_norm.reshape(1, -1).astype(F32), w_uq]
    q_cols = MLA_HEADS * MLA_QK_W if has_kv else MLA_Q_COLS
    out_specs, out_shape = [row(q_cols)], [jax.ShapeDtypeStruct((m, q_cols), BF16)]
    if has_kv:
        in_specs.append(full(w_ukv.shape))
        args.append(w_ukv)
        out_specs.append(row(MLA_HEADS * MLA_KV_W))
        out_shape.append(jax.ShapeDtypeStruct((m, MLA_HEADS * MLA_KV_W), BF16))
    rope_blocks = rope[0].shape[0] // tm
    in_specs += [pl.BlockSpec((tm, LANES), lambda i: (i % rope_blocks, 0))] * 2
    args += list(rope)
    out_specs += [row(MLA_KV_LORA), row(LANES)]
    out_shape += [jax.ShapeDtypeStruct((m, MLA_KV_LORA), F32), jax.ShapeDtypeStruct((m, LANES), F32)]
    return pl.pallas_call(
        functools.partial(_mla_proj_kernel, has_kv=has_kv),
        grid=(m // tm,),
        in_specs=in_specs,
        out_specs=out_specs,
        out_shape=out_shape,
        compiler_params=_params("parallel"),
        name="mla_project",
    )(*args)


def _mla_attn_kernel(q_ref, kv_ref, o_ref, *, tq):
    qi = pl.program_id(2)
    causal = (lax.broadcasted_iota(jnp.int32, (tq, tq), 1) <= lax.broadcasted_iota(jnp.int32, (tq, tq), 0))

    def tile(j, carry, on_diagonal):
        r0 = pl.multiple_of(j * tq, tq)
        out = []
        for e in range(2):
            m_prev, l_prev, acc = carry[e]
            k = kv_ref[pl.ds(r0, tq), pl.ds(e * MLA_KV_W, MLA_QK_W)]
            v = kv_ref[pl.ds(r0, tq), pl.ds(e * MLA_KV_W + MLA_QK_W, MLA_V)]
            s = _dot_nt(q_ref[:, e * MLA_QK_W:(e + 1) * MLA_QK_W], k)
            if on_diagonal:
                s = jnp.where(causal, s, NEG_BIG)
            m_new = jnp.maximum(m_prev, jnp.max(s, axis=-1, keepdims=True))
            alpha = jnp.exp2(m_prev - m_new)
            p = jnp.exp2(s - m_new)
            out.append((m_new, alpha * l_prev + jnp.sum(p, axis=-1, keepdims=True),
                        alpha * acc + _dot(p.astype(BF16), v)))
        return tuple(out)

    init = tuple((jnp.full((tq, 1), NEG_BIG, F32), jnp.zeros((tq, 1), F32), jnp.zeros((tq, MLA_V), F32))
                 for _ in range(2))
    carry = lax.fori_loop(0, qi, lambda j, c: tile(j, c, False), init)
    carry = tile(qi, carry, True)
    for e in range(2):
        o_ref[:, e * MLA_V:(e + 1) * MLA_V] = (carry[e][2] / carry[e][1]).astype(o_ref.dtype)


def _mla_prompt_attn(q, kv, batch, seq, tq):
    nq = seq // tq
    return pl.pallas_call(
        functools.partial(_mla_attn_kernel, tq=tq),
        grid=(batch, MLA_HEADS // 2, nq),
        in_specs=[
            pl.BlockSpec((tq, 2 * MLA_QK_W), lambda b, p, qi: (b * nq + qi, p)),
            pl.BlockSpec((seq, 2 * MLA_KV_W), lambda b, p, qi: (b, p)),
        ],
        out_specs=pl.BlockSpec((tq, 2 * MLA_V), lambda b, p, qi: (b * nq + qi, p)),
        out_shape=jax.ShapeDtypeStruct((batch * seq, MLA_HEADS * MLA_V), BF16),
        compiler_params=_params("parallel", "parallel", "arbitrary"),
        name="mla_prompt_attn",
    )(q, kv)


def _head_mm_kernel(x_ref, w_ref, o_ref, *, transpose_w):
    y = _dot_nt(x_ref[...], w_ref[...]) if transpose_w else _dot(x_ref[...], w_ref[...])
    o_ref[...] = y.astype(o_ref.dtype)


def _head_matmul(x, w_ukv, transpose_w):
    bsz = x.shape[0]
    xw = x.shape[1] // MLA_HEADS
    ow = MLA_KV_LORA if transpose_w else MLA_V
    return pl.pallas_call(
        functools.partial(_head_mm_kernel, transpose_w=transpose_w),
        grid=(MLA_HEADS,),
        in_specs=[
            pl.BlockSpec((bsz, xw), lambda h: (0, h)),
            pl.BlockSpec((MLA_KV_LORA, LANES), lambda h: (0, 2 * h if transpose_w else 2 * h + 1)),
        ],
        out_specs=pl.BlockSpec((bsz, ow), lambda h: (0, h)),
        out_shape=jax.ShapeDtypeStruct((bsz, MLA_HEADS * ow), BF16),
        compiler_params=_params("parallel"),
        name="mla_head_matmul",
    )(x, w_ukv)


MLA_DECODE_STREAMS = 2
MLA_DECODE_SLOTS = 3


def _mla_decode_kernel(pt_ref, ql_ref, qr_ref, ckv_ref, krn_ref, lat_hbm, rope_hbm, o_ref,
                       lat_buf, rope_buf, sem, m_ref, l_ref, acc_ref, *, layer, pages):
    b, c = pl.program_id(0), pl.program_id(1)
    n_chunks = pl.num_programs(1)
    n_steps = pl.num_programs(0) * n_chunks
    step = b * n_chunks + c
    n_slots = lat_buf.shape[0]
    ahead = n_slots - 1
    slot = step % n_slots
    page = rope_hbm.shape[3]
    ns = MLA_DECODE_STREAMS
    keys = pages * page // ns

    def page_copies(step_idx, slot_idx):
        bb, cc = step_idx // n_chunks, step_idx % n_chunks
        out = []
        for g in range(pages):
            pg = pt_ref[bb, cc * pages + g]
            out.append(pltpu.make_async_copy(lat_hbm.at[layer, pg], lat_buf.at[slot_idx, pl.ds(g * page, page), :],
                                             sem.at[0, slot_idx]))
            out.append(pltpu.make_async_copy(rope_hbm.at[layer, pg], rope_buf.at[slot_idx, :, pl.ds(g * page, page)],
                                             sem.at[1, slot_idx]))
        return out

    @pl.when(step == 0)
    def _():
        for first in range(ahead):
            for cp in page_copies(first, first):
                cp.start()

    @pl.when(c == 0)
    def _():
        m_ref[...] = jnp.full_like(m_ref, NEG_BIG)
        l_ref[...] = jnp.zeros_like(l_ref)
        acc_ref[...] = jnp.zeros_like(acc_ref)

    @pl.when(step + ahead < n_steps)
    def _():
        for cp in page_copies(step + ahead, (step + ahead) % n_slots):
            cp.start()

    for cp in page_copies(step, slot):
        cp.wait()

    ql, qr = ql_ref[0], qr_ref[0]
    lats, scores = [], []
    for st in range(ns):
        lat = lat_buf[slot, pl.ds(st * keys, keys), :].astype(BF16)
        rope_t = rope_buf[slot, :, pl.ds(st * keys, keys)].astype(BF16)
        lats.append(lat)
        scores.append(_dot_nt(ql, lat) + _dot(qr, rope_t))
    probs = []
    for st in range(ns):
        m_prev = m_ref[st]
        m_new = jnp.maximum(m_prev, jnp.max(scores[st], axis=-1, keepdims=True))
        alpha = jnp.exp2(m_prev - m_new)
        p = jnp.exp2(scores[st] - m_new)
        l_ref[st] = alpha * l_ref[st] + jnp.sum(p, axis=-1, keepdims=True)
        m_ref[st] = m_new
        probs.append((alpha, p.astype(BF16)))
    for st in range(ns):
        acc_ref[st] = probs[st][0] * acc_ref[st] + _dot(probs[st][1], lats[st])

    @pl.when(c == pl.num_programs(1) - 1)
    def _():
        ckv, krn = ckv_ref[0], krn_ref[0]
        s_new = (jnp.sum(ql.astype(F32) * ckv, axis=-1, keepdims=True)
                 + jnp.sum(qr.astype(F32) * krn, axis=-1, keepdims=True))
        m_fin = s_new
        for st in range(ns):
            m_fin = jnp.maximum(m_fin, m_ref[st])
        p_new = jnp.exp2(s_new - m_fin)
        l_fin = p_new
        acc = p_new * ckv
        for st in range(ns):
            w = jnp.exp2(m_ref[st] - m_fin)
            l_fin = l_fin + w * l_ref[st]
            acc = acc + w * acc_ref[st]
        o_ref[0] = (acc / l_fin).astype(o_ref.dtype)


def _mla_decode(q_lat, q_rope, ckv_new, kr_new, lat_cache, rope_cache, page_table, layer, pages):
    bsz, n_pages = page_table.shape
    page = lat_cache.shape[2]
    nh = q_lat.shape[1]
    ns = MLA_DECODE_STREAMS
    assert pages % ns == 0 and n_pages % pages == 0
    assert bsz * (n_pages // pages) >= MLA_DECODE_SLOTS - 1
    rope_cache_t = jnp.swapaxes(rope_cache, 2, 3)

    grid_spec = pltpu.PrefetchScalarGridSpec(
        num_scalar_prefetch=1,
        grid=(bsz, n_pages // pages),
        in_specs=[
            pl.BlockSpec((1, nh, MLA_KV_LORA), lambda b, c, pt: (b, 0, 0)),
            pl.BlockSpec((1, nh, MLA_ROPE), lambda b, c, pt: (b, 0, 0)),
            pl.BlockSpec((1, 1, MLA_KV_LORA), lambda b, c, pt: (b, 0, 0)),
            pl.BlockSpec((1, 1, MLA_ROPE), lambda b, c, pt: (b, 0, 0)),
            pl.BlockSpec(memory_space=pl.ANY),
            pl.BlockSpec(memory_space=pl.ANY),
        ],
        out_specs=pl.BlockSpec((1, nh, MLA_KV_LORA), lambda b, c, pt: (b, 0, 0)),
        scratch_shapes=[
            pltpu.VMEM((MLA_DECODE_SLOTS, pages * page, MLA_KV_LORA), F32),
            pltpu.VMEM((MLA_DECODE_SLOTS, MLA_ROPE, pages * page), F32),
            pltpu.SemaphoreType.DMA((2, MLA_DECODE_SLOTS)),
            pltpu.VMEM((ns, nh, 1), F32), pltpu.VMEM((ns, nh, 1), F32), pltpu.VMEM((ns, nh, MLA_KV_LORA), F32),
        ],
    )
    return pl.pallas_call(
        functools.partial(_mla_decode_kernel, layer=layer, pages=pages),
        grid_spec=grid_spec,
        out_shape=jax.ShapeDtypeStruct((bsz, nh, MLA_KV_LORA), BF16),
        compiler_params=_params("arbitrary", "arbitrary"),
        name="mla_decode",
    )(page_table, q_lat, q_rope, ckv_new, kr_new, lat_cache, rope_cache_t)


PROMPT_TM = 512
FFN_TK = 1024
OUT_TK = 2048
SWA_QKV_TN = 2560
GDN_IN_TM = 1024
GDN_IN_TN = 2048
GDN_ROWS = 256
MLA_TQ = 1024
MLA_PAGES_PER_STEP = 32
GDN_CONV_TR = 512
GDN_CONV_TC = 1024


def _swa_weights(w_qkv, b_qkv, n_heads):
    q_dim = n_heads * SWA_HEAD_DIM
    col_scale = jnp.where(jnp.arange(w_qkv.shape[1]) < q_dim, SWA_HEAD_DIM ** -0.5, 1.0).astype(F32)
    return (w_qkv * col_scale).astype(BF16), b_qkv * col_scale


def _swa_layer(a_p, a_s, rope_p, rope_s, w_qkv, b_qkv, sinks, ck, cv, batch, seq):
    n_heads = sinks.shape[0]
    kvh = ck.shape[2]
    q_dim, kv_dim = n_heads * SWA_HEAD_DIM, kvh * SWA_HEAD_DIM
    w, b = _swa_weights(w_qkv, b_qkv, n_heads)
    rope_cols = (0, q_dim + kv_dim)
    qkv_p = _matmul(a_p, w, PROMPT_TM, SWA_QKV_TN, bias=b, rope=rope_p, rope_cols=rope_cols, name="swa_qkv")
    o_p = _swa_prompt_attn(qkv_p, sinks, batch, seq)
    rows = min(SWA_WINDOW, seq)
    kv_tail = qkv_p.reshape(batch, seq, -1)[:, seq - rows:, q_dim:].astype(F32)
    k_p = kv_tail[..., :kv_dim].reshape(batch, rows, kvh, SWA_HEAD_DIM)
    v_p = kv_tail[..., kv_dim:].reshape(batch, rows, kvh, SWA_HEAD_DIM)
    qkv_s = _matmul(a_s, w, a_s.shape[0], 512, bias=b, rope=rope_s, rope_cols=rope_cols, out_dtype=F32,
                    name="swa_qkv")
    o_s, k_s, v_s = _swa_sample_attn(qkv_s, ck, cv, sinks)
    return o_p, o_s, (k_p, v_p, k_s, v_s)


def _gdn_layer(a_p, a_s, w_in, conv_w, a_log, dt_bias, norm_w, ssm, conv_state, batch, seq):
    n_v = a_log.shape[0]
    n_qk = n_v // GDN_V_PER_QK
    key_dim = n_qk * GDN_HEAD_DIM
    conv_dim = 2 * key_dim + n_v * GDN_HEAD_DIM
    main_cols = conv_dim + n_v * GDN_HEAD_DIM
    w_main = w_in.astype(BF16)
    w_ba = jnp.pad(w_in[:, main_cols:], ((0, 0), (0, LANES - 2 * n_v))).astype(BF16)
    proj_p = _matmul(a_p, w_main, GDN_IN_TM, GDN_IN_TN, n_cols=main_cols, name="gdn_in")
    ba_p = _matmul(a_p, w_ba, PROMPT_TM, LANES, out_dtype=F32, name="gdn_in_gates")
    beta_p, _, gc_p = _gdn_gates(ba_p, a_log, dt_bias, GDN_ROWS, GDN_CHUNK)
    qkv_p = _gdn_conv(proj_p, conv_w, seq, key_dim, conv_dim, GDN_CONV_TR, GDN_CONV_TC)
    y_p, s_p = _gdn_scan(qkv_p, proj_p, gc_p, beta_p, gc_p[:, :n_v].T, norm_w, batch, seq, n_qk, GDN_ROWS, GDN_CHUNK)
    taps = GDN_CONV - 1
    c_p = proj_p.reshape(batch, seq, -1)[:, seq - taps:, :conv_dim].astype(F32)
    bsz = a_s.shape[0]
    proj_s = _matmul(a_s, w_main, bsz, 1024, out_dtype=F32, n_cols=main_cols, name="gdn_in")
    ba_s = _matmul(a_s, w_ba, bsz, LANES, out_dtype=F32, name="gdn_in_gates")
    beta_s, g_s, _ = _gdn_gates(ba_s, a_log, dt_bias, bsz, 1)
    qkv_s = _gdn_conv_step(conv_state, proj_s, conv_w, key_dim, 1024)
    y_s, s_s = _gdn_step(qkv_s, proj_s, g_s, beta_s, norm_w, ssm, n_qk)
    c_s = jnp.concatenate([conv_state[:, 1:], proj_s[:, None, :conv_dim]], axis=1)
    return y_p, y_s, (s_p, c_p, s_s, c_s)


def _mla_weights(w_in, w_uq):
    zeros = jnp.zeros((w_in.shape[0], MLA_ROPE), w_in.dtype)
    kr = w_in[:, MLA_Q_LORA + MLA_KV_LORA:]
    w_in_p = jnp.concatenate([w_in[:, :MLA_Q_LORA + MLA_KV_LORA], kr, zeros, zeros, kr], axis=1)
    wq = w_uq.reshape(MLA_Q_LORA, MLA_HEADS, MLA_NOPE + MLA_ROPE) * ((MLA_NOPE + MLA_ROPE) ** -0.5 * LOG2E)
    wq = jnp.concatenate([wq[..., :MLA_NOPE].reshape(MLA_Q_LORA, -1), wq[..., MLA_NOPE:].reshape(MLA_Q_LORA, -1)],
                         axis=1)
    return w_in_p.astype(BF16), wq.astype(BF16)


def _mla_layer(a_p, a_s, rope_p, rope_s, w_in, q_norm, w_uq, kv_norm, w_ukv, lat_cache, rope_cache, page_table,
               layer, batch, seq):
    w_in_p, wq = _mla_weights(w_in, w_uq)
    w_ukv = w_ukv.astype(BF16)
    nope = MLA_HEADS * MLA_NOPE
    q_p, kv_p, ckv_p, kr_p = _mla_project(a_p, w_in_p, q_norm, wq, kv_norm, rope_p, 256, w_ukv=w_ukv)
    o_p = _mla_prompt_attn(q_p, kv_p, batch, seq, min(MLA_TQ, seq))
    bsz = a_s.shape[0]
    q_s, ckv_s, kr_s = _mla_project(a_s, w_in_p, q_norm, wq, kv_norm, rope_s, bsz)
    q_lat = _head_matmul(q_s[:, :nope], w_ukv, True).reshape(bsz, MLA_HEADS, MLA_KV_LORA)
    q_rope = q_s[:, nope:].reshape(bsz, MLA_HEADS, MLA_ROPE)
    kr_new = kr_s[:, :MLA_ROPE]
    o_lat = _mla_decode(q_lat, q_rope, ckv_s.reshape(bsz, 1, -1), kr_new.reshape(bsz, 1, -1), lat_cache, rope_cache,
                        page_table, layer, MLA_PAGES_PER_STEP)
    o_s = _head_matmul(o_lat.reshape(bsz, -1), w_ukv, False)
    outs = (ckv_p.reshape(batch, seq, -1), kr_p[:, :MLA_ROPE].reshape(batch, seq, -1),
            ckv_s.reshape(bsz, 1, -1), kr_new.reshape(bsz, 1, -1))
    return o_p, o_s, outs


def kernel(x_prompt, x_sample, cache_swa_k, cache_swa_v, state_gdn_ssm, state_gdn_conv, cache_mla_latent, cache_mla_krope, page_table, norm_mix_pre, norm_mix_post, norm_ffn_pre, norm_ffn_post, ffn_w_up, ffn_w_down, swa_w_qkv, swa_b_qkv, swa_w_o, swa_sinks, gdn_w_in, gdn_conv_w, gdn_A_log, gdn_dt_bias, gdn_norm, gdn_w_out, mla_w_in, mla_q_norm, mla_w_uq, mla_kv_norm, mla_w_ukv, mla_w_o):
    batch, seq, d_model = x_prompt.shape
    bsz, dec_seq, _ = x_sample.shape
    assert dec_seq == 1
    depth = norm_mix_pre.shape[0]
    past_len = page_table.shape[1] * cache_mla_latent.shape[2]
    hp = x_prompt.reshape(batch * seq, d_model)
    hs = x_sample.reshape(bsz, d_model)
    rope_p = _rope_tables(jnp.arange(seq, dtype=jnp.int32))
    rope_s = _rope_tables(jnp.full((bsz,), past_len, jnp.int32))
    a_p = _rmsnorm(hp, norm_mix_pre[0], PROMPT_TM)
    a_s = _rmsnorm(hs, norm_mix_pre[0], bsz)
    collected = ([], [], [])
    w_up, w_down = ffn_w_up.astype(BF16), ffn_w_down.astype(BF16)
    w_outs = (swa_w_o.astype(BF16), gdn_w_out.astype(BF16), mla_w_o.astype(BF16))
    for layer in range(depth):
        kind, j = layer % 3, layer // 3
        if kind == 0:
            x_p, x_s, outs = _swa_layer(a_p, a_s, rope_p, rope_s, swa_w_qkv[j], swa_b_qkv[j], swa_sinks[j],
                                        cache_swa_k[j], cache_swa_v[j], batch, seq)
        elif kind == 1:
            x_p, x_s, outs = _gdn_layer(a_p, a_s, gdn_w_in[j], gdn_conv_w[j], gdn_A_log[j], gdn_dt_bias[j],
                                        gdn_norm[j], state_gdn_ssm[j], state_gdn_conv[j], batch, seq)
        else:
            x_p, x_s, outs = _mla_layer(a_p, a_s, rope_p, rope_s, mla_w_in[j], mla_q_norm[j], mla_w_uq[j],
                                        mla_kv_norm[j], mla_w_ukv[j], cache_mla_latent, cache_mla_krope,
                                        page_table, j, batch, seq)
        collected[kind].append(outs)
        w_o = w_outs[kind]
        g_next = norm_mix_pre[layer + 1] if layer + 1 < depth else None
        hp, a_p = _proj_res(x_p, w_o, hp, norm_mix_post[layer], norm_ffn_pre[layer], PROMPT_TM, OUT_TK,
                            layer=j, name="mixer_out")
        hs, a_s = _proj_res(x_s, w_o, hs, norm_mix_post[layer], norm_ffn_pre[layer], bsz, OUT_TK,
                            layer=j, name="mixer_out")
        hp, a_p = _proj_res(a_p, w_down, hp, norm_ffn_post[layer], g_next, PROMPT_TM, FFN_TK, w_up=w_up,
                            layer=layer, name="ffn")
        hs, a_s = _proj_res(a_s, w_down, hs, norm_ffn_post[layer], g_next, bsz, FFN_TK, w_up=w_up,
                            layer=layer, name="ffn")

    def stacked(kind):
        return tuple(jnp.stack(parts) for parts in zip(*collected[kind]))

    return ((hp.reshape(batch, seq, d_model), hs.reshape(bsz, 1, d_model))
            + stacked(0) + stacked(1) + stacked(2))
```

```python
import functools
import math

import jax
import jax.numpy as jnp
from jax import lax
from jax.experimental import pallas as pl
from jax.experimental.pallas import tpu as pltpu

F32 = jnp.float32
BF16 = jnp.bfloat16

RMS_EPS = 1e-6
ROPE_THETA = 10000.0
NEG_BIG = -1e30

LANES = 128
VMEM_LIMIT_BYTES = 56 * 1024 * 1024

SWA_HEAD_DIM = 64
SWA_GROUP = 8
SWA_WINDOW = 128
GDN_HEAD_DIM = 128
GDN_V_PER_QK = 2
GDN_CONV = 4
GDN_CHUNK = 64
MLA_HEADS = 16
MLA_Q_LORA = 512
MLA_KV_LORA = 512
MLA_NOPE = 128
MLA_ROPE = 64
MLA_V = 128


def _params(*sem):
    return pltpu.CompilerParams(dimension_semantics=sem, vmem_limit_bytes=VMEM_LIMIT_BYTES)


def _rms(y, g):
    return y * lax.rsqrt(jnp.mean(y * y, axis=-1, keepdims=True) + RMS_EPS) * g


def _dot(a, b):
    return jnp.dot(a, b, preferred_element_type=F32)


def _dot_nt(a, b):
    return lax.dot_general(a, b, (((1,), (1,)), ((), ())), preferred_element_type=F32)


def _dot_tn(a, b):
    return lax.dot_general(a, b, (((0,), (0,)), ((), ())), preferred_element_type=F32)


def _bmm(a, b):
    return jnp.einsum("nmk,nkp->nmp", a, b, preferred_element_type=F32)


def _bmm_nt(a, b):
    return jnp.einsum("nmk,npk->nmp", a, b, preferred_element_type=F32)


def _sigmoid(x):
    return 1.0 / (1.0 + jnp.exp(-x))


def _silu(x):
    return x * _sigmoid(x)


def _softplus(x):
    return jnp.maximum(x, 0.0) + jnp.log(1.0 + jnp.exp(-jnp.abs(x)))


def _rope_block(y, cos, sin_signed):
    lane = lax.broadcasted_iota(jnp.int32, (1, LANES), 1)
    first = (lane % 64) < 32
    swapped = jnp.where(first, pltpu.roll(y, 96, 1), pltpu.roll(y, 32, 1))
    return y * cos + swapped * sin_signed


def _rope_tables(pos):
    half = SWA_HEAD_DIM // 2
    inv_freq = ROPE_THETA ** (-jnp.arange(half, dtype=F32) * (2.0 / SWA_HEAD_DIM))
    ang = pos.astype(F32)[:, None] * inv_freq[None, :]
    cos, sin = jnp.cos(ang), jnp.sin(ang)
    return jnp.tile(cos, (1, 4)), jnp.tile(jnp.concatenate([-sin, sin], axis=1), (1, 2))


def _rmsnorm_kernel(x_ref, g_ref, o_ref):
    o_ref[...] = _rms(x_ref[...], g_ref[...]).astype(o_ref.dtype)


def _rmsnorm(x, g, tm):
    m, d = x.shape
    return pl.pallas_call(
        _rmsnorm_kernel,
        grid=(m // tm,),
        in_specs=[pl.BlockSpec((tm, d), lambda i: (i, 0)), pl.BlockSpec((1, d), lambda i: (0, 0))],
        out_specs=pl.BlockSpec((tm, d), lambda i: (i, 0)),
        out_shape=jax.ShapeDtypeStruct((m, d), BF16),
        compiler_params=_params("parallel"),
        name="rmsnorm",
    )(x, g.reshape(1, d))


def _mm_kernel(*refs, has_bias, rope_cols, tn):
    a_ref, w_ref = refs[0], refs[1]
    idx = 2
    if has_bias:
        b_ref = refs[idx]
        idx += 1
    if rope_cols is not None:
        cos_ref, sin_ref = refs[idx], refs[idx + 1]
        idx += 2
    o_ref = refs[idx]
    y = _dot(a_ref[...], w_ref[...])
    if has_bias:
        y = y + b_ref[...]
    if rope_cols is None:
        o_ref[...] = y.astype(o_ref.dtype)
        return
    j = pl.program_id(1)
    cos, sin = cos_ref[...], sin_ref[...]
    for c in range(tn // LANES):
        col0 = j * tn + c * LANES
        on = (col0 >= rope_cols[0]) & (col0 < rope_cols[1])
        yc = y[:, c * LANES:(c + 1) * LANES]
        o_ref[:, c * LANES:(c + 1) * LANES] = jnp.where(on, _rope_block(yc, cos, sin), yc).astype(o_ref.dtype)


def _matmul(a, w, tm, tn, bias=None, rope=None, rope_cols=None, out_dtype=BF16, n_cols=None, name="matmul"):
    m, k = a.shape
    n = w.shape[1] if n_cols is None else n_cols
    assert m % tm == 0 and n % tn == 0 and n <= w.shape[1], (m, tm, n, tn)
    in_specs = [pl.BlockSpec((tm, k), lambda i, j: (i, 0)), pl.BlockSpec((k, tn), lambda i, j: (0, j))]
    args = [a, w]
    if bias is not None:
        in_specs.append(pl.BlockSpec((1, tn), lambda i, j: (0, j)))
        args.append(bias.reshape(1, n).astype(F32))
    if rope is not None:
        rope_blocks = rope[0].shape[0] // tm
        in_specs += [pl.BlockSpec((tm, LANES), lambda i, j: (i % rope_blocks, 0))] * 2
        args += list(rope)
    return pl.pallas_call(
        functools.partial(_mm_kernel, has_bias=bias is not None, rope_cols=rope_cols, tn=tn),
        grid=(m // tm, n // tn),
        in_specs=in_specs,
        out_specs=pl.BlockSpec((tm, tn), lambda i, j: (i, j)),
        out_shape=jax.ShapeDtypeStruct((m, n), out_dtype),
        compiler_params=_params("parallel", "arbitrary"),
        name=name,
    )(*args)


PROJ_PIECE = 512


def _proj_res_kernel(*refs, has_up, has_next, nk):
    if has_up:
        x_ref, wu_ref, w_ref, h_ref, gp_ref = refs[:5]
        idx = 5
    else:
        x_ref, w_ref, h_ref, gp_ref = refs[:4]
        idx = 4
    if has_next:
        gn_ref = refs[idx]
        idx += 1
    ho_ref = refs[idx]
    idx += 1
    if has_next:
        ao_ref = refs[idx]
        idx += 1
    acc_ref = refs[idx] if nk > 1 else None

    def finish(y):
        hn = h_ref[...] + _rms(y, gp_ref[...])
        ho_ref[...] = hn
        if has_next:
            ao_ref[...] = _rms(hn, gn_ref[...]).astype(BF16)

    if has_up:
        tk = wu_ref.shape[1]
        piece = min(tk, PROJ_PIECE)
        ups = []
        for p in range(tk // piece):
            u = jnp.maximum(_dot(x_ref[...], wu_ref[:, p * piece:(p + 1) * piece]), 0.0)
            ups.append((u * u).astype(BF16))
        x = jnp.concatenate(ups, axis=1) if len(ups) > 1 else ups[0]
    else:
        x = x_ref[...]

    if nk == 1:
        finish(_dot(x, w_ref[...]))
        return
    k = pl.program_id(1)

    @pl.when(k == 0)
    def _():
        acc_ref[...] = jnp.zeros_like(acc_ref)

    d = w_ref.shape[1]
    piece = min(d, PROJ_PIECE)
    for n in range(d // piece):
        cols = slice(n * piece, (n + 1) * piece)
        acc_ref[:, cols] += _dot(x, w_ref[:, cols])

    @pl.when(k == nk - 1)
    def _():
        finish(acc_ref[...])


def _proj_res(x, w, h, g_post, g_next, tm, tk, w_up=None, layer=0, name="proj_res"):
    m, d = h.shape
    kdim = w.shape[1]
    nk = kdim // tk
    has_up, has_next = w_up is not None, g_next is not None
    if has_up:
        dk = x.shape[1]
        in_specs = [pl.BlockSpec((tm, dk), lambda i, k: (i, 0)),
                    pl.BlockSpec((None, dk, tk), lambda i, k: (layer, 0, k)),
                    pl.BlockSpec((None, tk, d), lambda i, k: (layer, k, 0))]
        args = [x, w_up, w]
    else:
        in_specs = [pl.BlockSpec((tm, tk), lambda i, k: (i, k)),
                    pl.BlockSpec((None, tk, d), lambda i, k: (layer, k, 0))]
        args = [x, w]
    row = pl.BlockSpec((tm, d), lambda i, k: (i, 0))
    vec = pl.BlockSpec((1, d), lambda i, k: (0, 0))
    in_specs += [row, vec]
    args += [h, g_post.reshape(1, d)]
    out_specs, out_shape = [row], [jax.ShapeDtypeStruct((m, d), F32)]
    if has_next:
        in_specs.append(vec)
        args.append(g_next.reshape(1, d))
        out_specs.append(row)
        out_shape.append(jax.ShapeDtypeStruct((m, d), BF16))
    outs = pl.pallas_call(
        functools.partial(_proj_res_kernel, has_up=has_up, has_next=has_next, nk=nk),
        grid=(m // tm, nk),
        in_specs=in_specs,
        out_specs=out_specs,
        out_shape=out_shape,
        scratch_shapes=[pltpu.VMEM((tm, d), F32)] if nk > 1 else [],
        compiler_params=_params("parallel", "arbitrary"),
        name=name,
    )(*args)
    return (outs[0], outs[1]) if has_next else (outs[0], None)


def _softmax_sink_unnorm(s, sink):
    m = jnp.maximum(jnp.max(s, axis=-1, keepdims=True), sink)
    p = jnp.exp(s - m)
    return p, 1.0 / (jnp.sum(p, axis=-1, keepdims=True) + jnp.exp(sink - m))


def _swa_prompt_kernel(sink_ref, q_ref, kp_ref, kc_ref, vp_ref, vc_ref, o_ref, *, nb, n_kvh):
    w = SWA_WINDOW
    n = pl.program_id(0) % nb
    lane = lax.broadcasted_iota(jnp.int32, (1, LANES), 1)
    pairs = SWA_GROUP // 2
    pair_w = pairs * LANES

    def pair_operand(p_ref, c_ref, kvh):
        blk0 = (kvh // 2) * LANES
        blk = jnp.concatenate([p_ref[:, blk0:blk0 + LANES], c_ref[:, blk0:blk0 + LANES]], axis=0).astype(F32)
        odd = kvh % 2 == 1
        own = jnp.where((lane >= 64) == odd, blk, 0.0)
        other = pltpu.roll(own, 64, 1)
        lo, hi = (other, own) if odd else (own, other)
        return jnp.concatenate([lo, hi], axis=0).astype(BF16)

    qi = lax.broadcasted_iota(jnp.int32, (w, 2 * w), 0)
    kj = lax.broadcasted_iota(jnp.int32, (w, 2 * w), 1)
    rel = qi + w - kj
    mask = (rel >= 0) & (rel < w) & ((n > 0) | (kj >= w))
    k2 = [pair_operand(kp_ref, kc_ref, kvh) for kvh in range(n_kvh)]
    v2 = [pair_operand(vp_ref, vc_ref, kvh) for kvh in range(n_kvh)]
    s2 = [[_dot_nt(q_ref[:, kvh * pair_w + p * LANES:kvh * pair_w + (p + 1) * LANES], k2[kvh])
           for p in range(pairs)] for kvh in range(n_kvh)]
    probs, scales = [], []
    for kvh in range(n_kvh):
        for p in range(pairs):
            ps, rs = [], []
            for e in range(2):
                s = jnp.where(mask, s2[kvh][p][:, e * 2 * w:(e + 1) * 2 * w], NEG_BIG)
                pe, re = _softmax_sink_unnorm(s, sink_ref[kvh * SWA_GROUP + 2 * p + e])
                ps.append(pe.astype(BF16))
                rs.append(re)
            probs.append(jnp.concatenate(ps, axis=1))
            scales.append(jnp.where(lane < 64, rs[0], rs[1]))
    for kvh in range(n_kvh):
        for p in range(pairs):
            i = kvh * pairs + p
            o_ref[:, i * LANES:(i + 1) * LANES] = (_dot(probs[i], v2[kvh]) * scales[i]).astype(o_ref.dtype)


def _swa_prompt_attn(qkv, sinks, batch, seq):
    w = SWA_WINDOW
    nb = seq // w
    n_heads = sinks.shape[0]
    kvh = n_heads // SWA_GROUP
    q_dim, kv_dim = n_heads * SWA_HEAD_DIM, kvh * SWA_HEAD_DIM
    kblk0 = q_dim // kv_dim
    vblk0 = kblk0 + 1

    def prev(i):
        return jnp.where(i % nb == 0, i, i - 1)

    return pl.pallas_call(
        functools.partial(_swa_prompt_kernel, nb=nb, n_kvh=kvh),
        grid=(batch * nb,),
        in_specs=[
            pl.BlockSpec(memory_space=pltpu.SMEM),
            pl.BlockSpec((w, q_dim), lambda i: (i, 0)),
            pl.BlockSpec((w, kv_dim), lambda i: (prev(i), kblk0)),
            pl.BlockSpec((w, kv_dim), lambda i: (i, kblk0)),
            pl.BlockSpec((w, kv_dim), lambda i: (prev(i), vblk0)),
            pl.BlockSpec((w, kv_dim), lambda i: (i, vblk0)),
        ],
        out_specs=pl.BlockSpec((w, q_dim), lambda i: (i, 0)),
        out_shape=jax.ShapeDtypeStruct((batch * seq, q_dim), BF16),
        compiler_params=_params("parallel"),
        name="swa_prompt_attn",
    )(sinks.astype(F32), qkv, qkv, qkv, qkv, qkv)


def _swa_sample_kernel(q_ref, kn_ref, vn_ref, ck_ref, cv_ref, sink_ref, o_ref, cko_ref, cvo_ref, *, bt, npair):
    p_rows = ck_ref.shape[1]
    grows = 2 * SWA_GROUP
    lane = lax.broadcasted_iota(jnp.int32, (1, 1, LANES), 2)
    row = lax.broadcasted_iota(jnp.int32, (1, p_rows, 1), 1)
    grp = lax.broadcasted_iota(jnp.int32, (1, grows, 1), 1)
    k_new, v_new = kn_ref[...], vn_ref[...]
    cko_ref[:, pl.ds(0, p_rows - 1), :] = ck_ref[:, pl.ds(1, p_rows - 1), :]
    cvo_ref[:, pl.ds(0, p_rows - 1), :] = cv_ref[:, pl.ds(1, p_rows - 1), :]
    cko_ref[:, pl.ds(p_rows - 1, 1), :] = k_new
    cvo_ref[:, pl.ds(p_rows - 1, 1), :] = v_new
    qs, kks, vvs, sinks = [], [], [], []
    for c in range(npair):
        sl = slice(c * LANES, (c + 1) * LANES)
        kks.append(jnp.where(row == 0, k_new[:, :, sl], ck_ref[:, :, sl]).astype(BF16))
        vvs.append(jnp.where(row == 0, v_new[:, :, sl], cv_ref[:, :, sl]).astype(BF16))
        q2 = q_ref[:, pl.ds(c * grows, grows), :]
        qs.append(jnp.where((lane >= 64) == (grp >= SWA_GROUP), q2, 0.0).astype(BF16))
        sinks.append(jnp.broadcast_to(sink_ref[c], (bt, grows, 1)))
    vv = jnp.concatenate(vvs, axis=0)
    s = _bmm_nt(jnp.concatenate(qs, axis=0), jnp.concatenate(kks, axis=0))
    pe, re = _softmax_sink_unnorm(s, jnp.concatenate(sinks, axis=0))
    o2 = _bmm(pe.astype(BF16), vv) * re
    for c in range(npair):
        oc = o2[c * bt:(c + 1) * bt]
        o_ref[:, pl.ds(c * SWA_GROUP, SWA_GROUP), :] = jnp.where(
            lane < 64, oc[:, :SWA_GROUP, :], oc[:, SWA_GROUP:, :]).astype(o_ref.dtype)


SWA_SAMPLE_BT = 16


def _swa_sample_attn(qkv, ck, cv, sinks):
    bsz, p_rows, kvh, hd = ck.shape
    assert p_rows == SWA_WINDOW and hd == SWA_HEAD_DIM
    n_heads = kvh * SWA_GROUP
    q_dim, kv_dim = n_heads * hd, kvh * hd
    npair = kvh // 2
    bt = SWA_SAMPLE_BT
    q = qkv[:, :q_dim].reshape(bsz, npair, 2, SWA_GROUP, hd).transpose(0, 1, 3, 2, 4)
    q = jnp.broadcast_to(q.reshape(bsz, npair, 1, SWA_GROUP, 2 * hd), (bsz, npair, 2, SWA_GROUP, 2 * hd))
    q = q.reshape(bsz, npair * 2 * SWA_GROUP, 2 * hd)
    k_new = qkv[:, q_dim:q_dim + kv_dim].reshape(bsz, 1, kv_dim)
    v_new = qkv[:, q_dim + kv_dim:].reshape(bsz, 1, kv_dim)
    sink = sinks.astype(F32).reshape(npair, 2 * SWA_GROUP, 1)
    o, cko, cvo = pl.pallas_call(
        functools.partial(_swa_sample_kernel, bt=bt, npair=npair),
        grid=(bsz // bt,),
        in_specs=[
            pl.BlockSpec((bt, npair * 2 * SWA_GROUP, LANES), lambda i: (i, 0, 0)),
            pl.BlockSpec((bt, 1, kv_dim), lambda i: (i, 0, 0)),
            pl.BlockSpec((bt, 1, kv_dim), lambda i: (i, 0, 0)),
            pl.BlockSpec((bt, p_rows, kv_dim), lambda i: (i, 0, 0)),
            pl.BlockSpec((bt, p_rows, kv_dim), lambda i: (i, 0, 0)),
            pl.BlockSpec((npair, 2 * SWA_GROUP, 1), lambda i: (0, 0, 0)),
        ],
        out_specs=[
            pl.BlockSpec((bt, npair * SWA_GROUP, LANES), lambda i: (i, 0, 0)),
            pl.BlockSpec((bt, p_rows, kv_dim), lambda i: (i, 0, 0)),
            pl.BlockSpec((bt, p_rows, kv_dim), lambda i: (i, 0, 0)),
        ],
        out_shape=[
            jax.ShapeDtypeStruct((bsz, npair * SWA_GROUP, LANES), BF16),
            jax.ShapeDtypeStruct((bsz, p_rows, kv_dim), F32),
            jax.ShapeDtypeStruct((bsz, p_rows, kv_dim), F32),
        ],
        compiler_params=_params("parallel"),
        name="swa_sample_attn",
    )(q, k_new, v_new, ck.reshape(bsz, p_rows, kv_dim), cv.reshape(bsz, p_rows, kv_dim), sink)
    o = o.reshape(bsz, npair, SWA_GROUP, 2, hd).transpose(0, 1, 3, 2, 4).reshape(bsz, q_dim)
    return o, cko.reshape(bsz, p_rows, kvh, hd), cvo.reshape(bsz, p_rows, kvh, hd)


def _gdn_gates_kernel(ba_ref, alog_ref, dt_ref, beta_ref, g_ref, gc_ref, *, chunk, nh):
    rows = ba_ref.shape[0]
    x = ba_ref[...]
    beta_ref[...] = _sigmoid(x)
    a = pltpu.roll(x, LANES - nh, 1)
    g = -jnp.exp(alog_ref[...]) * _softplus(a + dt_ref[...])
    g_ref[...] = g
    pos = lax.broadcasted_iota(jnp.int32, (rows, 1), 0) % chunk
    acc = g
    shift = 1
    while shift < chunk:
        acc = acc + jnp.where(pos >= shift, pltpu.roll(acc, shift, 0), 0.0)
        shift *= 2
    gc_ref[...] = acc


def _gdn_gates(ba, a_log, dt_bias, tr, chunk):
    m = ba.shape[0]
    nh = a_log.shape[0]
    row = pl.BlockSpec((tr, LANES), lambda i: (i, 0))
    vec = pl.BlockSpec((1, LANES), lambda i: (0, 0))

    def pad(x):
        return jnp.pad(x.astype(F32), (0, LANES - nh)).reshape(1, LANES)

    return pl.pallas_call(
        functools.partial(_gdn_gates_kernel, chunk=chunk, nh=nh),
        grid=(m // tr,),
        in_specs=[row, vec, vec],
        out_specs=[row, row, row],
        out_shape=[jax.ShapeDtypeStruct((m, LANES), F32)] * 3,
        compiler_params=_params("parallel"),
        name="gdn_gates",
    )(ba, pad(a_log), pad(dt_bias))


def _gdn_conv_kernel(x_ref, halo_ref, w_ref, o_ref, ext_ref, *, seq, key_dim, scale):
    tr, tc = x_ref.shape
    i, j = pl.program_id(0), pl.program_id(1)
    hb = halo_ref.shape[0]
    starts_seq = (i * tr) % seq == 0
    ext_ref[pl.ds(0, 8), :] = jnp.where(starts_seq, 0.0, halo_ref[pl.ds(hb - 8, 8), :].astype(F32))
    ext_ref[pl.ds(8, tr), :] = x_ref[...].astype(F32)
    acc = ext_ref[pl.ds(8 - (GDN_CONV - 1), tr), :] * w_ref[0:1, :]
    for t in range(1, GDN_CONV):
        acc = acc + ext_ref[pl.ds(8 - (GDN_CONV - 1) + t, tr), :] * w_ref[t:t + 1, :]
    c = _silu(acc)
    col0 = j * tc

    @pl.when(col0 >= 2 * key_dim)
    def _():
        o_ref[...] = c.astype(o_ref.dtype)

    @pl.when(col0 < 2 * key_dim)
    def _():
        gain = jnp.where(col0 < key_dim, scale, 1.0)
        for hblk in range(tc // LANES):
            ch = c[:, hblk * LANES:(hblk + 1) * LANES]
            nrm = ch * (lax.rsqrt(jnp.sum(ch * ch, axis=-1, keepdims=True) + RMS_EPS) * gain)
            o_ref[:, hblk * LANES:(hblk + 1) * LANES] = nrm.astype(o_ref.dtype)


def _gdn_conv(proj, conv_w, seq, key_dim, conv_dim, tr, tc):
    m = proj.shape[0]
    hb = 16
    return pl.pallas_call(
        functools.partial(_gdn_conv_kernel, seq=seq, key_dim=key_dim, scale=GDN_HEAD_DIM ** -0.5),
        grid=(m // tr, conv_dim // tc),
        in_specs=[
            pl.BlockSpec((tr, tc), lambda i, j: (i, j)),
            pl.BlockSpec((hb, tc), lambda i, j: (jnp.maximum(i * (tr // hb) - 1, 0), j)),
            pl.BlockSpec((GDN_CONV, tc), lambda i, j: (0, j)),
        ],
        out_specs=pl.BlockSpec((tr, tc), lambda i, j: (i, j)),
        out_shape=jax.ShapeDtypeStruct((m, conv_dim), BF16),
        scratch_shapes=[pltpu.VMEM((tr + 8, tc), F32)],
        compiler_params=_params("parallel", "arbitrary"),
        name="gdn_conv",
    )(proj, proj, conv_w.astype(F32))


GDN_INV_BASE = 2
GDN_QK_PER_STEP = 8


def _unit_lower_inverse(a):
    n, c, _ = a.shape
    base = GDN_INV_BASE
    ri = lax.broadcasted_iota(jnp.int32, (1, c, c), 1)
    ci = lax.broadcasted_iota(jnp.int32, (1, c, c), 2)
    x = jnp.broadcast_to((ri == ci).astype(F32), (n, c, c))
    diag_blocks = jnp.where(ri // base == ci // base, a, 0.0)
    col_in_block = ci % base
    for j in range(base - 1):
        coef = jnp.sum(jnp.where(col_in_block == j, diag_blocks, 0.0), axis=2, keepdims=True)
        rows = jnp.concatenate(
            [jnp.broadcast_to(x[:, b * base + j:b * base + j + 1, :], (n, base, c)) for b in range(c // base)], axis=1)
        x = x - coef * rows
    size = base
    while size < c:
        off = jnp.where((ri // (2 * size) == ci // (2 * size)) & (ri // size != ci // size), a, 0.0)
        xb = x.astype(BF16)
        x = x - _bmm(_bmm(xb, off.astype(BF16)).astype(BF16), xb)
        size *= 2
    return x


def _gdn_scan_kernel(q_ref, k_ref, v_ref, z_ref, gc_ref, beta_ref, gt_ref, nw_ref, y_ref, so_ref, s_ref, *, chunk):
    rows = q_ref.shape[0]
    hd = GDN_HEAD_DIM
    nc = rows // chunk
    n_qk, per = GDN_QK_PER_STEP, GDN_V_PER_QK
    n_v = n_qk * per
    hq0, t = pl.program_id(1) * n_qk, pl.program_id(2)

    @pl.when(t == 0)
    def _():
        s_ref[...] = jnp.zeros_like(s_ref)

    lane = lax.broadcasted_iota(jnp.int32, (1, LANES), 1)
    ri = lax.broadcasted_iota(jnp.int32, (1, chunk, chunk), 1)
    ci = lax.broadcasted_iota(jnp.int32, (1, chunk, chunk), 2)
    tri, strict = ri >= ci, ri > ci
    q3 = [q_ref[:, h * hd:(h + 1) * hd].reshape(nc, chunk, hd) for h in range(n_qk)]
    k3 = [k_ref[:, h * hd:(h + 1) * hd].reshape(nc, chunk, hd) for h in range(n_qk)]
    gkk = [_bmm_nt(k3[h], k3[h]) for h in range(n_qk)]
    gqk = [_bmm_nt(q3[h], k3[h]) for h in range(n_qk)]
    bcol, eg, w_last, eg_last, lower, qk = [], [], [], [], [], []
    for i in range(n_v):
        pick = lane == hq0 * per + i
        gcol = jnp.sum(jnp.where(pick, gc_ref[...], 0.0), axis=1, keepdims=True).reshape(nc, chunk, 1)
        bcol.append(jnp.sum(jnp.where(pick, beta_ref[...], 0.0), axis=1, keepdims=True).reshape(nc, chunk, 1))
        grow_full = gt_ref[pl.ds(hq0 * per + i, 1), :]
        grow = jnp.stack([grow_full[:, c * chunk:(c + 1) * chunk] for c in range(nc)], axis=0)
        decay = jnp.where(tri, jnp.exp(jnp.where(tri, gcol - grow, 0.0)), 0.0)
        lower.append(jnp.where(strict, bcol[i] * gkk[i // per] * decay, 0.0))
        qk.append((gqk[i // per] * decay).astype(BF16))
        g_last = gcol[:, chunk - 1:chunk, :]
        eg.append(jnp.exp(gcol))
        w_last.append(jnp.exp(g_last - gcol))
        eg_last.append(jnp.exp(g_last))
    tinv = _unit_lower_inverse(jnp.concatenate(lower, axis=0)).astype(BF16)

    state = [s_ref[h] for h in range(n_qk)]
    for c in range(nc):
        r0 = c * chunk
        sb = [state[h].astype(BF16) for h in range(n_qk)]
        ks = [_dot(k3[h][c], sb[h]) for h in range(n_qk)]
        qs = [_dot(q3[h][c], sb[h]) for h in range(n_qk)]
        rhs = []
        for i in range(n_v):
            h, e = divmod(i, per)
            v = v_ref[pl.ds(r0, chunk), pl.ds(i * hd, hd)].astype(F32)
            rhs.append((bcol[i][c] * (v - eg[i][c] * ks[h][:, e * hd:(e + 1) * hd])).astype(BF16))
        v_new = _bmm(jnp.stack([tinv[i * nc + c] for i in range(n_v)]), jnp.stack(rhs))
        o_intra = _bmm(jnp.stack([qk[i][c] for i in range(n_v)]), v_new.astype(BF16))
        for i in range(n_v):
            h, e = divmod(i, per)
            o = eg[i][c] * qs[h][:, e * hd:(e + 1) * hd] + o_intra[i]
            gate = _silu(z_ref[pl.ds(r0, chunk), pl.ds(i * hd, hd)].astype(F32))
            y_ref[pl.ds(r0, chunk), pl.ds(i * hd, hd)] = (_rms(o, nw_ref[...]) * gate).astype(y_ref.dtype)
        for h in range(n_qk):
            mine = range(h * per, (h + 1) * per)
            wv = jnp.concatenate([(w_last[i][c] * v_new[i]).astype(BF16) for i in mine], axis=1)
            keep = jnp.concatenate([jnp.broadcast_to(eg_last[i][c], (1, hd)) for i in mine], axis=1)
            state[h] = keep * state[h] + _dot_tn(k3[h][c], wv)
    for h in range(n_qk):
        s_ref[h] = state[h]

    @pl.when(t == pl.num_programs(2) - 1)
    def _():
        for i in range(n_v):
            h, e = divmod(i, per)
            so_ref[0, i] = state[h][:, e * hd:(e + 1) * hd]


def _gdn_scan(qkv, proj, gc, beta, gc_t, norm_w, batch, seq, n_qk, rows, chunk):
    hd = GDN_HEAD_DIM
    m = batch * seq
    nt = seq // rows
    n_v = n_qk * GDN_V_PER_QK
    qw = GDN_QK_PER_STEP * hd
    n_local = GDN_QK_PER_STEP * GDN_V_PER_QK
    vw = n_local * hd
    k_blk0 = n_qk * hd // qw
    v_blk0 = 2 * n_qk * hd // vw
    z_blk0 = (2 * n_qk * hd + n_v * hd) // vw
    y, s_fin = pl.pallas_call(
        functools.partial(_gdn_scan_kernel, chunk=chunk),
        grid=(batch, n_qk // GDN_QK_PER_STEP, nt),
        in_specs=[
            pl.BlockSpec((rows, qw), lambda b, h, t: (b * nt + t, h)),
            pl.BlockSpec((rows, qw), lambda b, h, t: (b * nt + t, k_blk0 + h)),
            pl.BlockSpec((rows, vw), lambda b, h, t: (b * nt + t, v_blk0 + h)),
            pl.BlockSpec((rows, vw), lambda b, h, t: (b * nt + t, z_blk0 + h)),
            pl.BlockSpec((rows, LANES), lambda b, h, t: (b * nt + t, 0)),
            pl.BlockSpec((rows, LANES), lambda b, h, t: (b * nt + t, 0)),
            pl.BlockSpec((n_v, rows), lambda b, h, t: (0, b * nt + t)),
            pl.BlockSpec((1, hd), lambda b, h, t: (0, 0)),
        ],
        out_specs=[
            pl.BlockSpec((rows, vw), lambda b, h, t: (b * nt + t, h)),
            pl.BlockSpec((1, n_local, hd, hd), lambda b, h, t: (b, h, 0, 0)),
        ],
        out_shape=[
            jax.ShapeDtypeStruct((m, n_v * hd), BF16),
            jax.ShapeDtypeStruct((batch, n_v, hd, hd), F32),
        ],
        scratch_shapes=[pltpu.VMEM((GDN_QK_PER_STEP, hd, GDN_V_PER_QK * hd), F32)],
        compiler_params=_params("parallel", "parallel", "arbitrary"),
        name="gdn_scan",
    )(qkv, qkv, qkv, proj, gc, beta, gc_t, norm_w.reshape(1, hd).astype(F32))
    return y, s_fin


def _gdn_conv_step_kernel(s0_ref, s1_ref, s2_ref, x_ref, w_ref, o_ref, *, key_dim, scale):
    tc = x_ref.shape[1]
    acc = s0_ref[...] * w_ref[0:1, :] + s1_ref[...] * w_ref[1:2, :] + s2_ref[...] * w_ref[2:3, :] \
        + x_ref[...] * w_ref[3:4, :]
    c = _silu(acc)
    col0 = pl.program_id(0) * tc
    is_q = col0 < key_dim
    is_qk = col0 < 2 * key_dim
    for hblk in range(tc // LANES):
        ch = c[:, hblk * LANES:(hblk + 1) * LANES]
        nrm = ch * lax.rsqrt(jnp.sum(ch * ch, axis=-1, keepdims=True) + RMS_EPS)
        nrm = jnp.where(is_q, nrm * scale, nrm)
        o_ref[:, hblk * LANES:(hblk + 1) * LANES] = jnp.where(is_qk, nrm, ch)


def _gdn_conv_step(conv_state, mixed, conv_w, key_dim, tc):
    bsz, taps, cdim = conv_state.shape
    assert taps == GDN_CONV - 1
    ncol = cdim // tc
    flat = conv_state.reshape(bsz, taps * cdim)
    return pl.pallas_call(
        functools.partial(_gdn_conv_step_kernel, key_dim=key_dim, scale=GDN_HEAD_DIM ** -0.5),
        grid=(ncol,),
        in_specs=[
            pl.BlockSpec((bsz, tc), lambda j: (0, j)),
            pl.BlockSpec((bsz, tc), lambda j: (0, ncol + j)),
            pl.BlockSpec((bsz, tc), lambda j: (0, 2 * ncol + j)),
            pl.BlockSpec((bsz, tc), lambda j: (0, j)),
            pl.BlockSpec((GDN_CONV, tc), lambda j: (0, j)),
        ],
        out_specs=pl.BlockSpec((bsz, tc), lambda j: (0, j)),
        out_shape=jax.ShapeDtypeStruct((bsz, cdim), F32),
        compiler_params=_params("parallel"),
        name="gdn_conv_step",
    )(flat, flat, flat, mixed, conv_w.astype(F32))


def _gdn_step_kernel(q_ref, k_ref, v_ref, z_ref, g_ref, beta_ref, nw_ref, s_ref, y_ref, so_ref, o_scr):
    bsz = q_ref.shape[0]
    hv = pl.program_id(0)
    pick = lax.broadcasted_iota(jnp.int32, (1, LANES), 1) == hv
    egcol = jnp.exp(jnp.sum(jnp.where(pick, g_ref[...], 0.0), axis=1, keepdims=True))
    bcol = jnp.sum(jnp.where(pick, beta_ref[...], 0.0), axis=1, keepdims=True)
    q_t = q_ref[...].T
    k_t = k_ref[...].T
    for b in range(bsz):
        kc, qc = k_t[:, b:b + 1], q_t[:, b:b + 1]
        s1 = s_ref[b, 0] * egcol[b:b + 1, :]
        delta = (v_ref[b:b + 1, :] - jnp.sum(s1 * kc, axis=0, keepdims=True)) * bcol[b:b + 1, :]
        s2 = s1 + kc * delta
        so_ref[b, 0] = s2
        o_scr[b:b + 1, :] = jnp.sum(s2 * qc, axis=0, keepdims=True)
    y_ref[...] = (_rms(o_scr[...], nw_ref[...]) * _silu(z_ref[...])).astype(y_ref.dtype)


def _gdn_step(qkv, proj, g, beta, norm_w, ssm, n_qk):
    bsz, n_v, hd, _ = ssm.shape
    k_blk0, v_blk0 = n_qk, 2 * n_qk
    z_blk0 = 2 * n_qk + n_v
    row = pl.BlockSpec((bsz, LANES), lambda h: (0, 0))
    st = pl.BlockSpec((bsz, 1, hd, hd), lambda h: (0, h, 0, 0))
    return pl.pallas_call(
        _gdn_step_kernel,
        grid=(n_v,),
        in_specs=[
            pl.BlockSpec((bsz, hd), lambda h: (0, h // GDN_V_PER_QK)),
            pl.BlockSpec((bsz, hd), lambda h: (0, k_blk0 + h // GDN_V_PER_QK)),
            pl.BlockSpec((bsz, hd), lambda h: (0, v_blk0 + h)),
            pl.BlockSpec((bsz, hd), lambda h: (0, z_blk0 + h)),
            row, row,
            pl.BlockSpec((1, hd), lambda h: (0, 0)),
            st,
        ],
        out_specs=[pl.BlockSpec((bsz, hd), lambda h: (0, h)), st],
        out_shape=[jax.ShapeDtypeStruct((bsz, n_v * hd), BF16), jax.ShapeDtypeStruct(ssm.shape, F32)],
        scratch_shapes=[pltpu.VMEM((bsz, hd), F32)],
        compiler_params=_params("parallel"),
        name="gdn_step",
    )(qkv, qkv, qkv, proj, g, beta, norm_w.reshape(1, hd).astype(F32), ssm)


MLA_IN_COLS = MLA_Q_LORA + MLA_KV_LORA + 2 * LANES
MLA_Q_COLS = MLA_HEADS * (MLA_NOPE + MLA_ROPE)
MLA_QK_W = MLA_NOPE + LANES
MLA_KV_W = MLA_QK_W + MLA_V
LOG2E = math.log2(math.e)


def _mla_proj_kernel(*refs, has_kv):
    a_ref, win_ref, qn_ref, kvn_ref, wuq_ref = refs[:5]
    idx = 5
    if has_kv:
        wukv_ref = refs[idx]
        idx += 1
    cos_ref, sin_ref, q_ref = refs[idx:idx + 3]
    idx += 3
    if has_kv:
        kv_ref = refs[idx]
        idx += 1
    ckv_ref, kr_ref = refs[idx:idx + 2]
    cos, sin = cos_ref[...], sin_ref[...]
    p = _dot(a_ref[...], win_ref[...])
    cq = _rms(p[:, :MLA_Q_LORA], qn_ref[...]).astype(BF16)
    ckv = _rms(p[:, MLA_Q_LORA:MLA_Q_LORA + MLA_KV_LORA], kvn_ref[...])
    ckv_ref[...] = ckv
    r0 = MLA_Q_LORA + MLA_KV_LORA
    kr_ref[...] = _rope_block(p[:, r0:r0 + LANES], cos, sin)
    q = _dot(cq, wuq_ref[...])
    nope = MLA_HEADS * MLA_NOPE
    q_rope = [_rope_block(q[:, nope + c * LANES:nope + (c + 1) * LANES], cos, sin).astype(q_ref.dtype)
              for c in range(MLA_HEADS * MLA_ROPE // LANES)]
    if not has_kv:
        q_ref[:, :nope] = q[:, :nope].astype(q_ref.dtype)
        for c, qr in enumerate(q_rope):
            q_ref[:, nope + c * LANES:nope + (c + 1) * LANES] = qr
        return
    k_rope = [kr_ref[...].astype(kv_ref.dtype),
              _rope_block(p[:, r0 + LANES:r0 + 2 * LANES], cos, sin).astype(kv_ref.dtype)]
    kv = _dot(ckv.astype(BF16), wukv_ref[...])
    hd = MLA_NOPE
    for h in range(MLA_HEADS):
        q_ref[:, MLA_QK_W * h:MLA_QK_W * h + hd] = q[:, hd * h:hd * (h + 1)].astype(q_ref.dtype)
        q_ref[:, MLA_QK_W * h + hd:MLA_QK_W * (h + 1)] = q_rope[h // 2]
        k0 = MLA_KV_W * h
        kv_ref[:, k0:k0 + hd] = kv[:, 2 * hd * h:2 * hd * h + hd].astype(kv_ref.dtype)
        kv_ref[:, k0 + hd:k0 + MLA_QK_W] = k_rope[h % 2]
        kv_ref[:, k0 + MLA_QK_W:k0 + MLA_KV_W] = kv[:, 2 * hd * h + hd:2 * hd * (h + 1)].astype(kv_ref.dtype)


def _mla_project(a, w_in, q_norm, w_uq, kv_norm, rope, tm, w_ukv=None):
    m, d = a.shape
    has_kv = w_ukv is not None
    row = lambda n: pl.BlockSpec((tm, n), lambda i: (i, 0))
    full = lambda s: pl.BlockSpec(s, lambda i: (0, 0))
    in_specs = [row(d), full(w_in.shape), full((1, MLA_Q_LORA)), full((1, MLA_KV_LORA)), full(w_uq.shape)]
    args = [a, w_in, q_norm.reshape(1, -1).astype(F32), kv_norm.reshape(1, -1).astype(F32), w_uq]
    q_cols = MLA_HEADS * MLA_QK_W if has_kv else MLA_Q_COLS
    out_specs, out_shape = [row(q_cols)], [jax.ShapeDtypeStruct((m, q_cols), BF16)]
    if has_kv:
        in_specs.append(full(w_ukv.shape))
        args.append(w_ukv)
        out_specs.append(row(MLA_HEADS * MLA_KV_W))
        out_shape.append(jax.ShapeDtypeStruct((m, MLA_HEADS * MLA_KV_W), BF16))
    rope_blocks = rope[0].shape[0] // tm
    in_specs += [pl.BlockSpec((tm, LANES), lambda i: (i % rope_blocks, 0))] * 2
    args += list(rope)
    out_specs += [row(MLA_KV_LORA), row(LANES)]
    out_shape += [jax.ShapeDtypeStruct((m, MLA_KV_LORA), F32), jax.ShapeDtypeStruct((m, LANES), F32)]
    return pl.pallas_call(
        functools.partial(_mla_proj_kernel, has_kv=has_kv),
        grid=(m // tm,),
        in_specs=in_specs,
        out_specs=out_specs,
        out_shape=out_shape,
        compiler_params=_params("parallel"),
        name="mla_project",
    )(*args)


def _mla_attn_kernel(q_ref, kv_ref, o_ref, *, tq):
    qi = pl.program_id(2)
    causal = (lax.broadcasted_iota(jnp.int32, (tq, tq), 1) <= lax.broadcasted_iota(jnp.int32, (tq, tq), 0))

    def tile(j, carry, on_diagonal):
        r0 = pl.multiple_of(j * tq, tq)
        out = []
        for e in range(2):
            m_prev, l_prev, acc = carry[e]
            k = kv_ref[pl.ds(r0, tq), pl.ds(e * MLA_KV_W, MLA_QK_W)]
            v = kv_ref[pl.ds(r0, tq), pl.ds(e * MLA_KV_W + MLA_QK_W, MLA_V)]
            s = _dot_nt(q_ref[:, e * MLA_QK_W:(e + 1) * MLA_QK_W], k)
            if on_diagonal:
                s = jnp.where(causal, s, NEG_BIG)
            m_new = jnp.maximum(m_prev, jnp.max(s, axis=-1, keepdims=True))
            alpha = jnp.exp2(m_prev - m_new)
            p = jnp.exp2(s - m_new)
            out.append((m_new, alpha * l_prev + jnp.sum(p, axis=-1, keepdims=True),
                        alpha * acc + _dot(p.astype(BF16), v)))
        return tuple(out)

    init = tuple((jnp.full((tq, 1), NEG_BIG, F32), jnp.zeros((tq, 1), F32), jnp.zeros((tq, MLA_V), F32))
                 for _ in range(2))
    carry = lax.fori_loop(0, qi, lambda j, c: tile(j, c, False), init)
    carry = tile(qi, carry, True)
    for e in range(2):
        o_ref[:, e * MLA_V:(e + 1) * MLA_V] = (carry[e][2] / carry[e][1]).astype(o_ref.dtype)


def _mla_prompt_attn(q, kv, batch, seq, tq):
    nq = seq // tq
    return pl.pallas_call(
        functools.partial(_mla_attn_kernel, tq=tq),
        grid=(batch, MLA_HEADS // 2, nq),
        in_specs=[
            pl.BlockSpec((tq, 2 * MLA_QK_W), lambda b, p, qi: (b * nq + qi, p)),
            pl.BlockSpec((seq, 2 * MLA_KV_W), lambda b, p, qi: (b, p)),
        ],
        out_specs=pl.BlockSpec((tq, 2 * MLA_V), lambda b, p, qi: (b * nq + qi, p)),
        out_shape=jax.ShapeDtypeStruct((batch * seq, MLA_HEADS * MLA_V), BF16),
        compiler_params=_params("parallel", "parallel", "arbitrary"),
        name="mla_prompt_attn",
    )(q, kv)


def _head_mm_kernel(x_ref, w_ref, o_ref, *, transpose_w):
    y = _dot_nt(x_ref[...], w_ref[...]) if transpose_w else _dot(x_ref[...], w_ref[...])
    o_ref[...] = y.astype(o_ref.dtype)


def _head_matmul(x, w_ukv, transpose_w):
    bsz = x.shape[0]
    xw = x.shape[1] // MLA_HEADS
    ow = MLA_KV_LORA if transpose_w else MLA_V
    return pl.pallas_call(
        functools.partial(_head_mm_kernel, transpose_w=transpose_w),
        grid=(MLA_HEADS,),
        in_specs=[
            pl.BlockSpec((bsz, xw), lambda h: (0, h)),
            pl.BlockSpec((MLA_KV_LORA, LANES), lambda h: (0, 2 * h if transpose_w else 2 * h + 1)),
        ],
        out_specs=pl.BlockSpec((bsz, ow), lambda h: (0, h)),
        out_shape=jax.ShapeDtypeStruct((bsz, MLA_HEADS * ow), BF16),
        compiler_params=_params("parallel"),
        name="mla_head_matmul",
    )(x, w_ukv)


MLA_DECODE_STREAMS = 2
MLA_DECODE_SLOTS = 3


def _mla_decode_kernel(pt_ref, ql_ref, qr_ref, ckv_ref, krn_ref, lat_hbm, rope_hbm, o_ref,
                       lat_buf, rope_buf, sem, m_ref, l_ref, acc_ref, *, layer, pages):
    b, c = pl.program_id(0), pl.program_id(1)
    n_chunks = pl.num_programs(1)
    n_steps = pl.num_programs(0) * n_chunks
    step = b * n_chunks + c
    n_slots = lat_buf.shape[0]
    ahead = n_slots - 1
    slot = step % n_slots
    page = rope_hbm.shape[3]
    ns = MLA_DECODE_STREAMS
    keys = pages * page // ns

    def page_copies(step_idx, slot_idx):
        bb, cc = step_idx // n_chunks, step_idx % n_chunks
        out = []
        for g in range(pages):
            pg = pt_ref[bb, cc * pages + g]
            out.append(pltpu.make_async_copy(lat_hbm.at[layer, pg], lat_buf.at[slot_idx, pl.ds(g * page, page), :],
                                             sem.at[0, slot_idx]))
            out.append(pltpu.make_async_copy(rope_hbm.at[layer, pg], rope_buf.at[slot_idx, :, pl.ds(g * page, page)],
                                             sem.at[1, slot_idx]))
        return out

    @pl.when(step == 0)
    def _():
        for first in range(ahead):
            for cp in page_copies(first, first):
                cp.start()

    @pl.when(c == 0)
    def _():
        m_ref[...] = jnp.full_like(m_ref, NEG_BIG)
        l_ref[...] = jnp.zeros_like(l_ref)
        acc_ref[...] = jnp.zeros_like(acc_ref)

    @pl.when(step + ahead < n_steps)
    def _():
        for cp in page_copies(step + ahead, (step + ahead) % n_slots):
            cp.start()

    for cp in page_copies(step, slot):
        cp.wait()

    ql, qr = ql_ref[0], qr_ref[0]
    lats, scores = [], []
    for st in range(ns):
        lat = lat_buf[slot, pl.ds(st * keys, keys), :].astype(BF16)
        rope_t = rope_buf[slot, :, pl.ds(st * keys, keys)].astype(BF16)
        lats.append(lat)
        scores.append(_dot_nt(ql, lat) + _dot(qr, rope_t))
    probs = []
    for st in range(ns):
        m_prev = m_ref[st]
        m_new = jnp.maximum(m_prev, jnp.max(scores[st], axis=-1, keepdims=True))
        alpha = jnp.exp2(m_prev - m_new)
        p = jnp.exp2(scores[st] - m_new)
        l_ref[st] = alpha * l_ref[st] + jnp.sum(p, axis=-1, keepdims=True)
        m_ref[st] = m_new
        probs.append((alpha, p.astype(BF16)))
    for st in range(ns):
        acc_ref[st] = probs[st][0] * acc_ref[st] + _dot(probs[st][1], lats[st])

    @pl.when(c == pl.num_programs(1) - 1)
    def _():
        ckv, krn = ckv_ref[0], krn_ref[0]
        s_new = (jnp.sum(ql.astype(F32) * ckv, axis=-1, keepdims=True)
                 + jnp.sum(qr.astype(F32) * krn, axis=-1, keepdims=True))
        m_fin = s_new
        for st in range(ns):
            m_fin = jnp.maximum(m_fin, m_ref[st])
        p_new = jnp.exp2(s_new - m_fin)
        l_fin = p_new
        acc = p_new * ckv
        for st in range(ns):
            w = jnp.exp2(m_ref[st] - m_fin)
            l_fin = l_fin + w * l_ref[st]
            acc = acc + w * acc_ref[st]
        o_ref[0] = (acc / l_fin).astype(o_ref.dtype)


def _mla_decode(q_lat, q_rope, ckv_new, kr_new, lat_cache, rope_cache, page_table, layer, pages):
    bsz, n_pages = page_table.shape
    page = lat_cache.shape[2]
    nh = q_lat.shape[1]
    ns = MLA_DECODE_STREAMS
    assert pages % ns == 0 and n_pages % pages == 0
    assert bsz * (n_pages // pages) >= MLA_DECODE_SLOTS - 1
    rope_cache_t = jnp.swapaxes(rope_cache, 2, 3)

    grid_spec = pltpu.PrefetchScalarGridSpec(
        num_scalar_prefetch=1,
        grid=(bsz, n_pages // pages),
        in_specs=[
            pl.BlockSpec((1, nh, MLA_KV_LORA), lambda b, c, pt: (b, 0, 0)),
            pl.BlockSpec((1, nh, MLA_ROPE), lambda b, c, pt: (b, 0, 0)),
            pl.BlockSpec((1, 1, MLA_KV_LORA), lambda b, c, pt: (b, 0, 0)),
            pl.BlockSpec((1, 1, MLA_ROPE), lambda b, c, pt: (b, 0, 0)),
            pl.BlockSpec(memory_space=pl.ANY),
            pl.BlockSpec(memory_space=pl.ANY),
        ],
        out_specs=pl.BlockSpec((1, nh, MLA_KV_LORA), lambda b, c, pt: (b, 0, 0)),
        scratch_shapes=[
            pltpu.VMEM((MLA_DECODE_SLOTS, pages * page, MLA_KV_LORA), F32),
            pltpu.VMEM((MLA_DECODE_SLOTS, MLA_ROPE, pages * page), F32),
            pltpu.SemaphoreType.DMA((2, MLA_DECODE_SLOTS)),
            pltpu.VMEM((ns, nh, 1), F32), pltpu.VMEM((ns, nh, 1), F32), pltpu.VMEM((ns, nh, MLA_KV_LORA), F32),
        ],
    )
    return pl.pallas_call(
        functools.partial(_mla_decode_kernel, layer=layer, pages=pages),
        grid_spec=grid_spec,
        out_shape=jax.ShapeDtypeStruct((bsz, nh, MLA_KV_LORA), BF16),
        compiler_params=_params("arbitrary", "arbitrary"),
        name="mla_decode",
    )(page_table, q_lat, q_rope, ckv_new, kr_new, lat_cache, rope_cache_t)


PROMPT_TM = 512
FFN_TK = 1024
OUT_TK = 2048
SWA_QKV_TN = 2560
GDN_IN_TM = 1024
GDN_IN_TN = 2048
GDN_ROWS = 256
MLA_TQ = 1024
MLA_PAGES_PER_STEP = 32
GDN_CONV_TR = 512
GDN_CONV_TC = 1024


def _swa_weights(w_qkv, b_qkv, n_heads):
    q_dim = n_heads * SWA_HEAD_DIM
    col_scale = jnp.where(jnp.arange(w_qkv.shape[1]) < q_dim, SWA_HEAD_DIM ** -0.5, 1.0).astype(F32)
    return (w_qkv * col_scale).astype(BF16), b_qkv * col_scale


def _swa_layer(a_p, a_s, rope_p, rope_s, w_qkv, b_qkv, sinks, ck, cv, batch, seq):
    n_heads = sinks.shape[0]
    kvh = ck.shape[2]
    q_dim, kv_dim = n_heads * SWA_HEAD_DIM, kvh * SWA_HEAD_DIM
    w, b = _swa_weights(w_qkv, b_qkv, n_heads)
    rope_cols = (0, q_dim + kv_dim)
    qkv_p = _matmul(a_p, w, PROMPT_TM, SWA_QKV_TN, bias=b, rope=rope_p, rope_cols=rope_cols, name="swa_qkv")
    o_p = _swa_prompt_attn(qkv_p, sinks, batch, seq)
    rows = min(SWA_WINDOW, seq)
    kv_tail = qkv_p.reshape(batch, seq, -1)[:, seq - rows:, q_dim:].astype(F32)
    k_p = kv_tail[..., :kv_dim].reshape(batch, rows, kvh, SWA_HEAD_DIM)
    v_p = kv_tail[..., kv_dim:].reshape(batch, rows, kvh, SWA_HEAD_DIM)
    qkv_s = _matmul(a_s, w, a_s.shape[0], 512, bias=b, rope=rope_s, rope_cols=rope_cols, out_dtype=F32,
                    name="swa_qkv")
    o_s, k_s, v_s = _swa_sample_attn(qkv_s, ck, cv, sinks)
    return o_p, o_s, (k_p, v_p, k_s, v_s)


def _gdn_layer(a_p, a_s, w_in, conv_w, a_log, dt_bias, norm_w, ssm, conv_state, batch, seq):
    n_v = a_log.shape[0]
    n_qk = n_v // GDN_V_PER_QK
    key_dim = n_qk * GDN_HEAD_DIM
    conv_dim = 2 * key_dim + n_v * GDN_HEAD_DIM
    main_cols = conv_dim + n_v * GDN_HEAD_DIM
    w_main = w_in.astype(BF16)
    w_ba = jnp.pad(w_in[:, main_cols:], ((0, 0), (0, LANES - 2 * n_v))).astype(BF16)
    proj_p = _matmul(a_p, w_main, GDN_IN_TM, GDN_IN_TN, n_cols=main_cols, name="gdn_in")
    ba_p = _matmul(a_p, w_ba, PROMPT_TM, LANES, out_dtype=F32, name="gdn_in_gates")
    beta_p, _, gc_p = _gdn_gates(ba_p, a_log, dt_bias, GDN_ROWS, GDN_CHUNK)
    qkv_p = _gdn_conv(proj_p, conv_w, seq, key_dim, conv_dim, GDN_CONV_TR, GDN_CONV_TC)
    y_p, s_p = _gdn_scan(qkv_p, proj_p, gc_p, beta_p, gc_p[:, :n_v].T, norm_w, batch, seq, n_qk, GDN_ROWS, GDN_CHUNK)
    taps = GDN_CONV - 1
    c_p = proj_p.reshape(batch, seq, -1)[:, seq - taps:, :conv_dim].astype(F32)
    bsz = a_s.shape[0]
    proj_s = _matmul(a_s, w_main, bsz, 1024, out_dtype=F32, n_cols=main_cols, name="gdn_in")
    ba_s = _matmul(a_s, w_ba, bsz, LANES, out_dtype=F32, name="gdn_in_gates")
    beta_s, g_s, _ = _gdn_gates(ba_s, a_log, dt_bias, bsz, 1)
    qkv_s = _gdn_conv_step(conv_state, proj_s, conv_w, key_dim, 1024)
    y_s, s_s = _gdn_step(qkv_s, proj_s, g_s, beta_s, norm_w, ssm, n_qk)
    c_s = jnp.concatenate([conv_state[:, 1:], proj_s[:, None, :conv_dim]], axis=1)
    return y_p, y_s, (s_p, c_p, s_s, c_s)


def _mla_weights(w_in, w_uq):
    zeros = jnp.zeros((w_in.shape[0], MLA_ROPE), w_in.dtype)
    kr = w_in[:, MLA_Q_LORA + MLA_KV_LORA:]
    w_in_p = jnp.concatenate([w_in[:, :MLA_Q_LORA + MLA_KV_LORA], kr, zeros, zeros, kr], axis=1)
    wq = w_uq.reshape(MLA_Q_LORA, MLA_HEADS, MLA_NOPE + MLA_ROPE) * ((MLA_NOPE + MLA_ROPE) ** -0.5 * LOG2E)
    wq = jnp.concatenate([wq[..., :MLA_NOPE].reshape(MLA_Q_LORA, -1), wq[..., MLA_NOPE:].reshape(MLA_Q_LORA, -1)],
                         axis=1)
    return w_in_p.astype(BF16), wq.astype(BF16)


def _mla_layer(a_p, a_s, rope_p, rope_s, w_in, q_norm, w_uq, kv_norm, w_ukv, lat_cache, rope_cache, page_table,
               layer, batch, seq):
    w_in_p, wq = _mla_weights(w_in, w_uq)
    w_ukv = w_ukv.astype(BF16)
    nope = MLA_HEADS * MLA_NOPE
    q_p, kv_p, ckv_p, kr_p = _mla_project(a_p, w_in_p, q_norm, wq, kv_norm, rope_p, 256, w_ukv=w_ukv)
    o_p = _mla_prompt_attn(q_p, kv_p, batch, seq, min(MLA_TQ, seq))
    bsz = a_s.shape[0]
    q_s, ckv_s, kr_s = _mla_project(a_s, w_in_p, q_norm, wq, kv_norm, rope_s, bsz)
    q_lat = _head_matmul(q_s[:, :nope], w_ukv, True).reshape(bsz, MLA_HEADS, MLA_KV_LORA)
    q_rope = q_s[:, nope:].reshape(bsz, MLA_HEADS, MLA_ROPE)
    kr_new = kr_s[:, :MLA_ROPE]
    o_lat = _mla_decode(q_lat, q_rope, ckv_s.reshape(bsz, 1, -1), kr_new.reshape(bsz, 1, -1), lat_cache, rope_cache,
                        page_table, layer, MLA_PAGES_PER_STEP)
    o_s = _head_matmul(o_lat.reshape(bsz, -1), w_ukv, False)
    outs = (ckv_p.reshape(batch, seq, -1), kr_p[:, :MLA_ROPE].reshape(batch, seq, -1),
            ckv_s.reshape(bsz, 1, -1), kr_new.reshape(bsz, 1, -1))
    return o_p, o_s, outs


def kernel(x_prompt, x_sample, cache_swa_k, cache_swa_v, state_gdn_ssm, state_gdn_conv, cache_mla_latent, cache_mla_krope, page_table, norm_mix_pre, norm_mix_post, norm_ffn_pre, norm_ffn_post, ffn_w_up, ffn_w_down, swa_w_qkv, swa_b_qkv, swa_w_o, swa_sinks, gdn_w_in, gdn_conv_w, gdn_A_log, gdn_dt_bias, gdn_norm, gdn_w_out, mla_w_in, mla_q_norm, mla_w_uq, mla_kv_norm, mla_w_ukv, mla_w_o):
    batch, seq, d_model = x_prompt.shape
    bsz, dec_seq, _ = x_sample.shape
    assert dec_seq == 1
    depth = norm_mix_pre.shape[0]
    past_len = page_table.shape[1] * cache_mla_latent.shape[2]
    hp = x_prompt.reshape(batch * seq, d_model)
    hs = x_sample.reshape(bsz, d_model)
    rope_p = _rope_tables(jnp.arange(seq, dtype=jnp.int32))
    rope_s = _rope_tables(jnp.full((bsz,), past_len, jnp.int32))
    a_p = _rmsnorm(hp, norm_mix_pre[0], PROMPT_TM)
    a_s = _rmsnorm(hs, norm_mix_pre[0], bsz)
    collected = ([], [], [])
    w_up, w_down = ffn_w_up.astype(BF16), ffn_w_down.astype(BF16)
    w_outs = (swa_w_o.astype(BF16), gdn_w_out.astype(BF16), mla_w_o.astype(BF16))
    for layer in range(depth):
        kind, j = layer % 3, layer // 3
        if kind == 0:
            x_p, x_s, outs = _swa_layer(a_p, a_s, rope_p, rope_s, swa_w_qkv[j], swa_b_qkv[j], swa_sinks[j],
                                        cache_swa_k[j], cache_swa_v[j], batch, seq)
        elif kind == 1:
            x_p, x_s, outs = _gdn_layer(a_p, a_s, gdn_w_in[j], gdn_conv_w[j], gdn_A_log[j], gdn_dt_bias[j],
                                        gdn_norm[j], state_gdn_ssm[j], state_gdn_conv[j], batch, seq)
        else:
            x_p, x_s, outs = _mla_layer(a_p, a_s, rope_p, rope_s, mla_w_in[j], mla_q_norm[j], mla_w_uq[j],
                                        mla_kv_norm[j], mla_w_ukv[j], cache_mla_latent, cache_mla_krope,
                                        page_table, j, batch, seq)
        collected[kind].append(outs)
        w_o = w_outs[kind]
        g_next = norm_mix_pre[layer + 1] if layer + 1 < depth else None
        hp, a_p = _proj_res(x_p, w_o, hp, norm_mix_post[layer], norm_ffn_pre[layer], PROMPT_TM, OUT_TK,
                            layer=j, name="mixer_out")
        hs, a_s = _proj_res(x_s, w_o, hs, norm_mix_post[layer], norm_ffn_pre[layer], bsz, OUT_TK,
                            layer=j, name="mixer_out")
        hp, a_p = _proj_res(a_p, w_down, hp, norm_ffn_post[layer], g_next, PROMPT_TM, FFN_TK, w_up=w_up,
                            layer=layer, name="ffn")
        hs, a_s = _proj_res(a_s, w_down, hs, norm_ffn_post[layer], g_next, bsz, FFN_TK, w_up=w_up,
                            layer=layer, name="ffn")

    def stacked(kind):
        return tuple(jnp.stack(parts) for parts in zip(*collected[kind]))

    return ((hp.reshape(batch, seq, d_model), hs.reshape(bsz, 1, d_model))
            + stacked(0) + stacked(1) + stacked(2))
```
